```python
import math, functools
import jax, jax.numpy as jnp
from jax import lax
import numpy as np

D_MODEL = 2048
BATCH = 4
SEQ = 2048
DEPTH = 4
DEC_BATCH = 8
DEC_SEQ = 8
PAST_LEN = 16384
PAGE_SIZE = 128

N_META = 16
HEAD_DIM = 128
DIFF_HEADS = 4
DIFF_VDIM = 2 * HEAD_DIM
DIFF_QK = DIFF_HEADS * 2 * HEAD_DIM
DIFF_WIDTH = DIFF_HEADS * DIFF_VDIM
DN_HEADS = 8
DN_DK = 128
DN_DV = 128
DN_QK = DN_HEADS * DN_DK
DN_WIDTH = DN_HEADS * DN_DV
MIX_WIDTH = DIFF_WIDTH + DN_WIDTH
CONV_K = 4
CHUNK = 64
Q_BLOCK = 128
D_FF = 4 * D_MODEL
EPS = 1e-6
IN_SIZES = (DIFF_QK, DIFF_QK, DIFF_WIDTH, DN_QK, DN_QK, DN_WIDTH, DN_WIDTH, DN_HEADS, DN_HEADS)
IN_COLS = 2 * DIFF_QK + DIFF_WIDTH + 2 * DN_QK + 2 * DN_WIDTH + 2 * DN_HEADS

kernel_name = 'hybrid_diffattn_gdn_decoder_step'


def rmsnorm(x, g):
    xf = x.astype(jnp.float32)
    y = xf * lax.rsqrt(jnp.mean(xf * xf, axis=-1, keepdims=True) + EPS)
    return (y * g.astype(jnp.float32)).astype(x.dtype)


def l2norm(x):
    return x * lax.rsqrt(jnp.sum(x * x, axis=-1, keepdims=True) + EPS)


def split_proj(p):
    out, off = [], 0
    for s in IN_SIZES:
        out.append(p[..., off:off + s])
        off += s
    return out


def causal_conv(buf, x, w):
    T = x.shape[1]
    xp = jnp.concatenate([buf.astype(x.dtype), x], axis=1)
    y = xp[:, 0:T] * w[0]
    for i in range(1, CONV_K):
        y = y + xp[:, i:i + T] * w[i]
    return y, xp[:, xp.shape[1] - (CONV_K - 1):]


def diff_attend(q, k, v, q_pos, k_pos, lam):
    s = jnp.einsum('bqhcd,bkhcd->bhcqk', q, k, preferred_element_type=jnp.float32) * (HEAD_DIM ** -0.5)
    s = jnp.where(k_pos[None, :] <= q_pos[:, None], s, -jnp.inf)
    p = jax.nn.softmax(s, axis=-1)
    a = p[:, :, 0] - lam * p[:, :, 1]
    return jnp.einsum('bhqk,bkhe->bqhe', a.astype(v.dtype), v)


def diff_attend_prompt(q, k, v, lam):
    B, L = q.shape[:2]
    pos = jnp.arange(L)
    o_meta = diff_attend(q[:, :N_META], k[:, :N_META], v[:, :N_META], pos[:N_META], pos[:N_META], lam)
    nb = (L - N_META) // Q_BLOCK
    qb = jnp.moveaxis(q[:, N_META:].reshape(B, nb, Q_BLOCK, *q.shape[2:]), 1, 0)
    starts = N_META + Q_BLOCK * jnp.arange(nb)

    def block(args):
        qi, s0 = args
        return diff_attend(qi, k, v, s0 + jnp.arange(Q_BLOCK), pos, lam)

    ob = lax.map(block, (qb, starts))
    ob = jnp.moveaxis(ob, 0, 1).reshape(B, L - N_META, *ob.shape[3:])
    return jnp.concatenate([o_meta, ob], axis=1)


def diff_attend_sample(past_k, past_v, q, k, v, lam):
    B, T = q.shape[:2]
    P = past_k.shape[1]
    k_all = jnp.concatenate([past_k.astype(k.dtype).reshape(B, P, DIFF_HEADS, 2, HEAD_DIM), k], axis=1)
    v_all = jnp.concatenate([past_v.astype(v.dtype), v], axis=1)
    return diff_attend(q, k_all, v_all, P + jnp.arange(T), jnp.arange(P + T), lam)


def delta_chunk(S, q, k, v, g, beta):
    S = S.astype(jnp.float32)
    C = q.shape[2]
    G = jnp.cumsum(g, axis=-1)
    causal = jnp.tril(jnp.ones((C, C), dtype=bool))
    strict = jnp.tril(jnp.ones((C, C), dtype=bool), -1)
    decay = jnp.exp(jnp.where(causal, G[..., :, None] - G[..., None, :], -jnp.inf))
    kk = jnp.einsum('bhid,bhjd->bhij', k, k)
    A = jnp.eye(C, dtype=jnp.float32) + beta[..., :, None] * jnp.where(strict, kk * decay, 0.0)
    eG = jnp.exp(G)
    rhs = beta[..., None] * (v - eG[..., None] * jnp.einsum('bhid,bhde->bhie', k, S))
    u = lax.linalg.triangular_solve(A, rhs, left_side=True, lower=True, unit_diagonal=True)
    qk = jnp.einsum('bhid,bhjd->bhij', q, k) * decay
    o = eG[..., None] * jnp.einsum('bhid,bhde->bhie', q, S) + jnp.einsum('bhij,bhje->bhie', qk, u)
    tail = jnp.exp(G[..., -1:] - G)
    S_new = jnp.exp(G[..., -1])[..., None, None] * S + jnp.einsum('bhjd,bhje->bhde', k * tail[..., None], u)
    return S_new, o


def dn_prompt(q, k, v, g, beta):
    B, H = q.shape[:2]
    S0 = jnp.zeros((B, H, DN_DK, DN_DV), jnp.float32)
    S1, o_meta = delta_chunk(S0, q[:, :, :N_META], k[:, :, :N_META], v[:, :, :N_META],
                             g[:, :, :N_META], beta[:, :, :N_META])

    def chunks(t):
        t = t[:, :, N_META:]
        n = t.shape[2] // CHUNK
        return jnp.moveaxis(t.reshape(B, H, n, CHUNK, *t.shape[3:]), 2, 0)

    xs = (chunks(q), chunks(k), chunks(v), chunks(g), chunks(beta))
    S_fin, oc = lax.scan(lambda S, c: delta_chunk(S, *c), S1, xs)
    oc = jnp.moveaxis(oc, 0, 2).reshape(B, H, -1, DN_DV)
    return S_fin, jnp.concatenate([o_meta, oc], axis=2)


def trunk_layer(x, conv_buf, attend, dn_run, l, diff_lambda_l, norm1_g_l, w_in_l, q_norm_g_l, k_norm_g_l,
                subln_g_l, conv_w_l, a_log_l, dt_bias_l, dn_norm_g_l, w_out_l, norm2_g_l, w_up_l, w_down_l):
    B, T, _ = x.shape
    f32 = jnp.float32
    lam_init = 0.8 - 0.6 * math.exp(-0.3 * l)
    lf = diff_lambda_l.astype(f32)
    lam = jnp.exp(jnp.sum(lf[0] * lf[1])) - jnp.exp(jnp.sum(lf[2] * lf[3])) + lam_init

    xn = rmsnorm(x, norm1_g_l)
    dq, dk, dv, nq, nk, nv, z, b, a = split_proj(xn @ w_in_l)

    q = rmsnorm(dq.reshape(B, T, DIFF_HEADS, 2, HEAD_DIM), q_norm_g_l)
    k = rmsnorm(dk.reshape(B, T, DIFF_HEADS, 2, HEAD_DIM), k_norm_g_l)
    v = dv.reshape(B, T, DIFF_HEADS, DIFF_VDIM)
    o_att = attend(q, k, v, lam)
    o_att = (rmsnorm(o_att, subln_g_l) * (1.0 - lam_init)).reshape(B, T, DIFF_WIDTH)

    qkv, conv_new = causal_conv(conv_buf, jnp.concatenate([nq, nk, nv], axis=-1), conv_w_l)
    qkv = jax.nn.silu(qkv).astype(f32)
    gq = l2norm(qkv[..., :DN_QK].reshape(B, T, DN_HEADS, DN_DK)) * (DN_DK ** -0.5)
    gk = l2norm(qkv[..., DN_QK:2 * DN_QK].reshape(B, T, DN_HEADS, DN_DK))
    gv = qkv[..., 2 * DN_QK:].reshape(B, T, DN_HEADS, DN_DV)
    beta = jax.nn.sigmoid(b.astype(f32))
    g = -jnp.exp(a_log_l.astype(f32)) * jax.nn.softplus(a.astype(f32) + dt_bias_l.astype(f32))
    tr = lambda t: jnp.moveaxis(t, 1, 2)
    S_fin, o_dn = dn_run(tr(gq), tr(gk), tr(gv), tr(g), tr(beta))
    o_dn = jnp.moveaxis(o_dn, 2, 1)
    o_dn = rmsnorm(o_dn, dn_norm_g_l) * jax.nn.silu(z.reshape(B, T, DN_HEADS, DN_DV).astype(f32))
    o_dn = o_dn.astype(x.dtype).reshape(B, T, DN_WIDTH)

    x = x + jnp.concatenate([o_att, o_dn], axis=-1) @ w_out_l
    h = jnp.square(jax.nn.relu(rmsnorm(x, norm2_g_l) @ w_up_l))
    x = x + h @ w_down_l
    new_k = k.reshape(B, T, DIFF_HEADS, 2 * HEAD_DIM)
    return x, (new_k, v, S_fin.astype(x.dtype), conv_new)


def setup_inputs(seed: int = 0) -> dict:
    key = jax.random.key(seed)
    ks = jax.random.split(key, 24)
    f32 = jnp.float32
    n_pages = PAST_LEN // PAGE_SIZE
    used = DEC_BATCH * n_pages
    n_pool = used + max(1, used // 4)
    nrm = lambda k, shape, s: s * jax.random.normal(k, shape, f32)
    gain = lambda k, shape: 1.0 + 0.02 * jax.random.normal(k, shape, f32)
    page_table = jax.random.permutation(ks[6], n_pool)[:used].reshape(DEC_BATCH, n_pages).astype(jnp.int32)
    return {
        'x_prompt': nrm(ks[0], (BATCH, SEQ, D_MODEL), 1.0),
        'x_sample': nrm(ks[1], (DEC_BATCH, DEC_SEQ, D_MODEL), 1.0),
        'cache_k': nrm(ks[2], (DEPTH, n_pool, PAGE_SIZE, DIFF_HEADS, 2 * HEAD_DIM), 1.0),
        'cache_v': nrm(ks[3], (DEPTH, n_pool, PAGE_SIZE, DIFF_HEADS, 2 * HEAD_DIM), 1.0),
        'state_ssm': nrm(ks[4], (DEPTH, DEC_BATCH, DN_HEADS, DN_DK, DN_DV), 0.5),
        'state_conv': nrm(ks[5], (DEPTH, DEC_BATCH, CONV_K - 1, 3 * DN_WIDTH), 1.0),
        'page_table': page_table,
        'meta_tokens': nrm(ks[7], (N_META, D_MODEL), 1.0),
        'norm1_g': gain(ks[8], (DEPTH, D_MODEL)),
        'w_in': nrm(ks[9], (DEPTH, D_MODEL, IN_COLS), D_MODEL ** -0.5),
        'q_norm_g': gain(ks[10], (DEPTH, HEAD_DIM)),
        'k_norm_g': gain(ks[11], (DEPTH, HEAD_DIM)),
        'diff_lambda': nrm(ks[12], (DEPTH, 4, HEAD_DIM), 0.1),
        'subln_g': gain(ks[13], (DEPTH, 2 * HEAD_DIM)),
        'conv_w': nrm(ks[14], (DEPTH, CONV_K, 3 * DN_WIDTH), 0.5),
        'a_log': jnp.log(jax.random.uniform(ks[15], (DEPTH, DN_HEADS), f32, 1.0, 16.0)),
        'dt_bias': nrm(ks[16], (DEPTH, DN_HEADS), 0.1),
        'dn_norm_g': gain(ks[17], (DEPTH, DN_DV)),
        'w_out': nrm(ks[18], (DEPTH, MIX_WIDTH, D_MODEL), MIX_WIDTH ** -0.5),
        'norm2_g': gain(ks[19], (DEPTH, D_MODEL)),
        'w_up': nrm(ks[20], (DEPTH, D_MODEL, D_FF), D_MODEL ** -0.5),
        'w_down': nrm(ks[21], (DEPTH, D_FF, D_MODEL), D_FF ** -0.5),
    }


def reference(x_prompt, x_sample, cache_k, cache_v, state_ssm, state_conv, page_table, meta_tokens,
              norm1_g, w_in, q_norm_g, k_norm_g, diff_lambda, subln_g, conv_w, a_log, dt_bias,
              dn_norm_g, w_out, norm2_g, w_up, w_down):
    B = x_prompt.shape[0]
    DB = x_sample.shape[0]
    past_len = page_table.shape[1] * PAGE_SIZE
    meta = jnp.broadcast_to(meta_tokens.astype(x_prompt.dtype)[None], (B, N_META, D_MODEL))
    xp = jnp.concatenate([meta, x_prompt], axis=1)
    xs = x_sample
    conv0 = jnp.zeros((B, CONV_K - 1, 3 * DN_WIDTH), x_prompt.dtype)
    kp, vp, ks_, vs_, sp, ss, cp, cs = [], [], [], [], [], [], [], []
    for l in range(DEPTH):
        w = (diff_lambda[l], norm1_g[l], w_in[l], q_norm_g[l], k_norm_g[l], subln_g[l], conv_w[l],
             a_log[l], dt_bias[l], dn_norm_g[l], w_out[l], norm2_g[l], w_up[l], w_down[l])
        xp, (k1, v1, s1, c1) = trunk_layer(xp, conv0, diff_attend_prompt, dn_prompt, l, *w)
        past_k = cache_k[l][page_table].reshape(DB, past_len, DIFF_HEADS, 2 * HEAD_DIM)
        past_v = cache_v[l][page_table].reshape(DB, past_len, DIFF_HEADS, 2 * HEAD_DIM)
        xs, (k2, v2, s2, c2) = trunk_layer(xs, state_conv[l],
                                           functools.partial(diff_attend_sample, past_k, past_v),
                                           functools.partial(delta_chunk, state_ssm[l]), l, *w)
        kp.append(k1); vp.append(v1); sp.append(s1); cp.append(c1)
        ks_.append(k2); vs_.append(v2); ss.append(s2); cs.append(c2)
    y_prompt = xp[:, N_META:]
    return (y_prompt, xs, jnp.stack(kp), jnp.stack(vp), jnp.stack(ks_), jnp.stack(vs_),
            jnp.stack(sp), jnp.stack(ss), jnp.stack(cp), jnp.stack(cs))
```

```python
import functools
import math

import jax
import jax.numpy as jnp
from jax import lax
from jax.experimental import pallas as pl
from jax.experimental.pallas import tpu as pltpu

F32 = jnp.float32
BF16 = jnp.bfloat16

D_MODEL = 2048
BATCH = 4
SEQ = 2048
DEPTH = 4
DEC_BATCH = 8
DEC_SEQ = 8
PAGE_SIZE = 128
N_META = 16
HEAD_DIM = 128
DIFF_HEADS = 4
DIFF_VDIM = 2 * HEAD_DIM
DIFF_WIDTH = DIFF_HEADS * DIFF_VDIM
DN_HEADS = 8
DN_DK = 128
DN_WIDTH = DN_HEADS * DN_DK
CONV_K = 4
D_FF = 4 * D_MODEL
EPS = 1e-6
MAIN_COLS = 3 * DIFF_WIDTH + 4 * DN_WIDTH

ROWS_PROMPT = BATCH * SEQ
ROWS_SAMPLE = DEC_BATCH * DEC_SEQ
ROW_SAMPLE0 = ROWS_PROMPT
ROW_META0 = ROWS_PROMPT + ROWS_SAMPLE
N_ROWS = 8320
TM = 640
LANES = 128
CHUNK = 128
NEG = -1e30
VMEM_LIMIT = 56 * 1024 * 1024


def _cparams(sem):
    return pltpu.CompilerParams(dimension_semantics=sem, vmem_limit_bytes=VMEM_LIMIT)


def _rms(x, gain):
    return x * lax.rsqrt(jnp.mean(x * x, axis=-1, keepdims=True) + EPS) * gain


def _in_proj_kernel(x_ref, g_ref, w_ref, wt_ref, qg_ref, kg_ref, o_ref, ot_ref, xn_ref, *, n_q, n_k):
    n = pl.program_id(1)

    @pl.when(n == 0)
    def _():
        xn_ref[...] = _rms(x_ref[...], g_ref[...]).astype(BF16)
        ot_ref[...] = jnp.dot(xn_ref[...], wt_ref[...], preferred_element_type=F32)

    acc = jnp.dot(xn_ref[...], w_ref[...], preferred_element_type=F32)

    def head_norm(gain):
        cols = [_rms(acc[:, j * HEAD_DIM:(j + 1) * HEAD_DIM], gain) for j in range(acc.shape[1] // HEAD_DIM)]
        return jnp.concatenate(cols, axis=1)

    @pl.when(n < n_q)
    def _():
        o_ref[...] = head_norm(qg_ref[...] * (HEAD_DIM ** -0.5))

    @pl.when(jnp.logical_and(n >= n_q, n < n_q + n_k))
    def _():
        o_ref[...] = head_norm(kg_ref[...])

    @pl.when(n >= n_q + n_k)
    def _():
        o_ref[...] = acc


def _in_proj(x, g, w_main, w_tail, qg, kg, l, tn=512):
    n_q = DIFF_WIDTH // tn
    return pl.pallas_call(
        functools.partial(_in_proj_kernel, n_q=n_q, n_k=n_q),
        grid=(N_ROWS // TM, MAIN_COLS // tn),
        in_specs=[
            pl.BlockSpec((TM, D_MODEL), lambda m, n: (m, 0)),
            pl.BlockSpec((None, 1, D_MODEL), lambda m, n: (l, 0, 0)),
            pl.BlockSpec((None, D_MODEL, tn), lambda m, n: (l, 0, n)),
            pl.BlockSpec((None, D_MODEL, LANES), lambda m, n: (l, 0, 0)),
            pl.BlockSpec((None, 1, HEAD_DIM), lambda m, n: (l, 0, 0)),
            pl.BlockSpec((None, 1, HEAD_DIM), lambda m, n: (l, 0, 0)),
        ],
        out_specs=[
            pl.BlockSpec((TM, tn), lambda m, n: (m, n)),
            pl.BlockSpec((TM, LANES), lambda m, n: (m, 0)),
        ],
        out_shape=[
            jax.ShapeDtypeStruct((N_ROWS, MAIN_COLS), F32),
            jax.ShapeDtypeStruct((N_ROWS, LANES), F32),
        ],
        scratch_shapes=[pltpu.VMEM((TM, D_MODEL), BF16)],
        compiler_params=_cparams(("parallel", "arbitrary")),
        name="in_proj",
    )(x, g, w_main, w_tail, qg, kg)


def _out_proj_kernel(mix_ref, w_ref, x_ref, o_ref):
    o_ref[...] = x_ref[...] + jnp.dot(mix_ref[...], w_ref[...], preferred_element_type=F32)


def _out_proj(mix, w_out, x, l, tn=512):
    k = mix.shape[1]
    return pl.pallas_call(
        _out_proj_kernel,
        grid=(N_ROWS // TM, D_MODEL // tn),
        in_specs=[
            pl.BlockSpec((TM, k), lambda m, n: (m, 0)),
            pl.BlockSpec((None, k, tn), lambda m, n: (l, 0, n)),
            pl.BlockSpec((TM, tn), lambda m, n: (m, n)),
        ],
        out_specs=pl.BlockSpec((TM, tn), lambda m, n: (m, n)),
        out_shape=jax.ShapeDtypeStruct((N_ROWS, D_MODEL), F32),
        compiler_params=_cparams(("parallel", "arbitrary")),
        name="out_proj",
    )(mix, w_out, x)


def _ffn_kernel(x_ref, g_ref, wu_ref, wd_ref, o_ref, xn_ref, acc_ref):
    f = pl.program_id(1)

    @pl.when(f == 0)
    def _():
        xn_ref[...] = _rms(x_ref[...], g_ref[...]).astype(BF16)
        acc_ref[...] = jnp.zeros_like(acc_ref)

    h = jnp.maximum(jnp.dot(xn_ref[...], wu_ref[...], preferred_element_type=F32), 0.0)
    acc_ref[...] += jnp.dot((h * h).astype(BF16), wd_ref[...], preferred_element_type=F32)

    @pl.when(f == pl.num_programs(1) - 1)
    def _():
        o_ref[...] = x_ref[...] + acc_ref[...]


def _ffn(x, g, w_up, w_down, l, tf=512):
    return pl.pallas_call(
        _ffn_kernel,
        grid=(N_ROWS // TM, D_FF // tf),
        in_specs=[
            pl.BlockSpec((TM, D_MODEL), lambda m, f: (m, 0)),
            pl.BlockSpec((None, 1, D_MODEL), lambda m, f: (l, 0, 0)),
            pl.BlockSpec((None, D_MODEL, tf), lambda m, f: (l, 0, f)),
            pl.BlockSpec((None, tf, D_MODEL), lambda m, f: (l, f, 0)),
        ],
        out_specs=pl.BlockSpec((TM, D_MODEL), lambda m, f: (m, 0)),
        out_shape=jax.ShapeDtypeStruct((N_ROWS, D_MODEL), F32),
        scratch_shapes=[pltpu.VMEM((TM, D_MODEL), BF16), pltpu.VMEM((TM, D_MODEL), F32)],
        compiler_params=_cparams(("parallel", "arbitrary")),
        name="ffn",
    )(x, g, w_up, w_down)


def _diff_lambda(lam_ref, lam_init):
    lf = lam_ref[...]
    s1 = jnp.sum(lf[0:1] * lf[1:2], axis=-1, keepdims=True)
    s2 = jnp.sum(lf[2:3] * lf[3:4], axis=-1, keepdims=True)
    return jnp.exp(s1) - jnp.exp(s2) + lam_init


def _online_softmax_step(s, v, m_ref, l_ref, acc_ref):
    m_prev = m_ref[...]
    m_new = jnp.maximum(m_prev, jnp.max(s, axis=-1, keepdims=True))
    alpha = jnp.exp(m_prev - m_new)
    p = jnp.exp(s - m_new)
    l_ref[...] = alpha * l_ref[...] + jnp.sum(p, axis=-1, keepdims=True)
    acc_ref[...] = alpha * acc_ref[...] + jnp.dot(p.astype(BF16), v, preferred_element_type=F32)
    m_ref[...] = m_new


def _diff_finish(o1, o2, lam, g, lam_init):
    return _rms(o1 - lam * o2, g) * (1.0 - lam_init)


def _block_diag_q(q, t):
    z = jnp.zeros((t, HEAD_DIM), q.dtype)
    top = jnp.concatenate([q[:, :HEAD_DIM], z], axis=1)
    bot = jnp.concatenate([z, q[:, HEAD_DIM:]], axis=1)
    return jnp.concatenate([top, bot], axis=0)


_NT = (((1,), (1,)), ((), ()))


def _prompt_attn_kernel(q_ref, k_ref, v_ref, km_ref, vm_ref, lam_ref, g_ref, o_ref,
                        kb_ref, vb_ref, m_ref, l_ref, acc_ref, *, tq, lam_init):
    qi = pl.program_id(2)

    @pl.when(qi == 0)
    def _():
        kb_ref[...] = k_ref[...].astype(BF16)
        vb_ref[...] = v_ref[...].astype(BF16)

    q = q_ref[...].astype(BF16)
    q1 = q[:, :HEAD_DIM]
    q2 = q[:, HEAD_DIM:]

    def scores(kblk):
        s1 = lax.dot_general(q1, kblk[:, :HEAD_DIM], _NT, preferred_element_type=F32)
        s2 = lax.dot_general(q2, kblk[:, HEAD_DIM:], _NT, preferred_element_type=F32)
        return jnp.concatenate([s1, s2], axis=0)

    m_ref[...] = jnp.full_like(m_ref, NEG)
    l_ref[...] = jnp.zeros_like(l_ref)
    acc_ref[...] = jnp.zeros_like(acc_ref)

    s = scores(km_ref[...].astype(BF16))
    col = lax.broadcasted_iota(jnp.int32, s.shape, 1)
    _online_softmax_step(jnp.where(col < N_META, s, NEG), vm_ref[...].astype(BF16), m_ref, l_ref, acc_ref)

    row = lax.broadcasted_iota(jnp.int32, (2 * tq, tq), 0)
    row = jnp.where(row >= tq, row - tq, row)
    col = lax.broadcasted_iota(jnp.int32, (2 * tq, tq), 1)

    def body(j, carry):
        off = pl.multiple_of(j * tq, tq)
        s = scores(kb_ref[pl.ds(off, tq), :])
        s = jnp.where(col + j * tq <= row + qi * tq, s, NEG)
        _online_softmax_step(s, vb_ref[pl.ds(off, tq), :], m_ref, l_ref, acc_ref)
        return carry

    lax.fori_loop(0, qi + 1, body, 0)

    o = acc_ref[...] / l_ref[...]
    lam = _diff_lambda(lam_ref, lam_init)
    o_ref[...] = _diff_finish(o[:tq], o[tq:], lam, g_ref[...], lam_init).astype(o_ref.dtype)


def _prompt_attn(proj, kmeta, vmeta, lam, subln_g, l, lam_init, tq=256):
    nq = SEQ // tq
    kcol0 = DIFF_WIDTH // DIFF_VDIM
    return pl.pallas_call(
        functools.partial(_prompt_attn_kernel, tq=tq, lam_init=lam_init),
        grid=(BATCH, DIFF_HEADS, nq),
        in_specs=[
            pl.BlockSpec((tq, DIFF_VDIM), lambda b, h, i: (b * nq + i, h)),
            pl.BlockSpec((SEQ, DIFF_VDIM), lambda b, h, i: (b, kcol0 + h)),
            pl.BlockSpec((SEQ, DIFF_VDIM), lambda b, h, i: (b, 2 * kcol0 + h)),
            pl.BlockSpec((LANES, DIFF_VDIM), lambda b, h, i: (0, h)),
            pl.BlockSpec((LANES, DIFF_VDIM), lambda b, h, i: (0, h)),
            pl.BlockSpec((None, 4, HEAD_DIM), lambda b, h, i: (l, 0, 0)),
            pl.BlockSpec((None, 1, DIFF_VDIM), lambda b, h, i: (l, 0, 0)),
        ],
        out_specs=pl.BlockSpec((tq, DIFF_VDIM), lambda b, h, i: (b * nq + i, h)),
        out_shape=jax.ShapeDtypeStruct((ROWS_PROMPT, DIFF_WIDTH), BF16),
        scratch_shapes=[
            pltpu.VMEM((SEQ, DIFF_VDIM), BF16),
            pltpu.VMEM((SEQ, DIFF_VDIM), BF16),
            pltpu.VMEM((2 * tq, 1), F32),
            pltpu.VMEM((2 * tq, 1), F32),
            pltpu.VMEM((2 * tq, DIFF_VDIM), F32),
        ],
        compiler_params=_cparams(("parallel", "parallel", "arbitrary")),
        name="prompt_attn",
    )(proj, proj, proj, kmeta, vmeta, lam, subln_g)


def _meta_attn_kernel(q_ref, km_ref, vm_ref, lam_ref, g_ref, o_ref, *, lam_init):
    t = N_META
    lam = _diff_lambda(lam_ref, lam_init)
    row = lax.broadcasted_iota(jnp.int32, (2 * t, LANES), 0)
    row = jnp.where(row >= t, row - t, row)
    col = lax.broadcasted_iota(jnp.int32, (2 * t, LANES), 1)
    for h in range(DIFF_HEADS):
        sl = slice(h * DIFF_VDIM, (h + 1) * DIFF_VDIM)
        qbd = _block_diag_q(q_ref[:, sl], t).astype(BF16)
        s = lax.dot_general(qbd, km_ref[:, sl].astype(BF16), _NT, preferred_element_type=F32)
        s = jnp.where(col <= row, s, NEG)
        p = jnp.exp(s - jnp.max(s, axis=-1, keepdims=True))
        o = jnp.dot(p.astype(BF16), vm_ref[:, sl].astype(BF16), preferred_element_type=F32)
        o = o / jnp.sum(p, axis=-1, keepdims=True)
        o_ref[:, sl] = _diff_finish(o[:t], o[t:], lam, g_ref[...], lam_init)


def _meta_attn(qmeta, kmeta, vmeta, lam, subln_g, l, lam_init):
    return pl.pallas_call(
        functools.partial(_meta_attn_kernel, lam_init=lam_init),
        grid=(1,),
        in_specs=[
            pl.BlockSpec((N_META, DIFF_WIDTH), lambda i: (0, 0)),
            pl.BlockSpec((LANES, DIFF_WIDTH), lambda i: (0, 0)),
            pl.BlockSpec((LANES, DIFF_WIDTH), lambda i: (0, 0)),
            pl.BlockSpec((None, 4, HEAD_DIM), lambda i: (l, 0, 0)),
            pl.BlockSpec((None, 1, DIFF_VDIM), lambda i: (l, 0, 0)),
        ],
        out_specs=pl.BlockSpec((N_META, DIFF_WIDTH), lambda i: (0, 0)),
        out_shape=jax.ShapeDtypeStruct((N_META, DIFF_WIDTH), F32),
        compiler_params=_cparams(("arbitrary",)),
        name="meta_attn",
    )(qmeta, kmeta, vmeta, lam, subln_g)


def _sample_attn_kernel(pt_ref, q_ref, kn_ref, vn_ref, *rest, pp, lam_init):
    kp_refs = rest[:pp]
    vp_refs = rest[pp:2 * pp]
    lam_ref, g_ref, o_ref, qbd_ref, m_ref, l_ref, acc_ref = rest[2 * pp:]
    t = DEC_SEQ
    g = pl.program_id(1)
    rows_h = 2 * t

    @pl.when(g == 0)
    def _():
        for h in range(DIFF_HEADS):
            qbd_ref[h] = _block_diag_q(q_ref[:, h * DIFF_VDIM:(h + 1) * DIFF_VDIM], t).astype(BF16)
        m_ref[...] = jnp.full_like(m_ref, NEG)
        l_ref[...] = jnp.zeros_like(l_ref)
        acc_ref[...] = jnp.zeros_like(acc_ref)

    def attend(k_blocks, v_blocks, mask):
        s = jnp.concatenate([
            jnp.concatenate([
                lax.dot_general(qbd_ref[h], kb[:, h * DIFF_VDIM:(h + 1) * DIFF_VDIM].astype(BF16), _NT,
                                preferred_element_type=F32)
                for kb in k_blocks], axis=1)
            for h in range(DIFF_HEADS)], axis=0)
        if mask is not None:
            s = jnp.where(mask, s, NEG)
        m_prev = m_ref[...]
        m_new = jnp.maximum(m_prev, jnp.max(s, axis=-1, keepdims=True))
        alpha = jnp.exp(m_prev - m_new)
        p = jnp.exp(s - m_new)
        l_ref[...] = alpha * l_ref[...] + jnp.sum(p, axis=-1, keepdims=True)
        m_ref[...] = m_new
        pb = p.astype(BF16)
        for h in range(DIFF_HEADS):
            rs = slice(h * rows_h, (h + 1) * rows_h)
            pv = None
            for i, vb in enumerate(v_blocks):
                keys = vb.shape[0]
                d = jnp.dot(pb[rs, i * keys:(i + 1) * keys], vb[:, h * DIFF_VDIM:(h + 1) * DIFF_VDIM].astype(BF16),
                            preferred_element_type=F32)
                pv = d if pv is None else pv + d
            acc_ref[rs, :] = alpha[rs] * acc_ref[rs, :] + pv

    attend([r[...] for r in kp_refs], [r[...] for r in vp_refs], None)

    @pl.when(g == pl.num_programs(1) - 1)
    def _():
        zpad = jnp.zeros((LANES - t, DIFF_WIDTH), F32)
        kn = jnp.concatenate([kn_ref[...], zpad], axis=0)
        vn = jnp.concatenate([vn_ref[...], zpad], axis=0)
        row = lax.broadcasted_iota(jnp.int32, (DIFF_HEADS * rows_h, LANES), 0)
        col = lax.broadcasted_iota(jnp.int32, (DIFF_HEADS * rows_h, LANES), 1)
        attend([kn], [vn], col <= row % t)
        lam = _diff_lambda(lam_ref, lam_init)
        o = acc_ref[...] / l_ref[...]
        for h in range(DIFF_HEADS):
            o1 = o[h * rows_h:h * rows_h + t]
            o2 = o[h * rows_h + t:(h + 1) * rows_h]
            o_ref[:, h * DIFF_VDIM:(h + 1) * DIFF_VDIM] = _diff_finish(o1, o2, lam, g_ref[...], lam_init)


def _sample_attn(page_table, proj, cache_k, cache_v, lam, subln_g, l, lam_init, pp=4):
    n_pages = page_table.shape[1]
    row0 = ROW_SAMPLE0 // DEC_SEQ
    qkv_spec = lambda c: pl.BlockSpec((DEC_SEQ, DIFF_WIDTH), lambda b, g, pt: (row0 + b, c))
    page_spec = lambda i: pl.BlockSpec((None, None, PAGE_SIZE, DIFF_WIDTH),
                                       lambda b, g, pt: (l, pt[b, g * pp + i], 0, 0))
    grid_spec = pltpu.PrefetchScalarGridSpec(
        num_scalar_prefetch=1,
        grid=(DEC_BATCH, n_pages // pp),
        in_specs=[qkv_spec(0), qkv_spec(1), qkv_spec(2)]
        + [page_spec(i) for i in range(pp)] + [page_spec(i) for i in range(pp)]
        + [pl.BlockSpec((None, 4, HEAD_DIM), lambda b, g, pt: (l, 0, 0)),
           pl.BlockSpec((None, 1, DIFF_VDIM), lambda b, g, pt: (l, 0, 0))],
        out_specs=pl.BlockSpec((DEC_SEQ, DIFF_WIDTH), lambda b, g, pt: (b, 0)),
        scratch_shapes=[
            pltpu.VMEM((DIFF_HEADS, 2 * DEC_SEQ, DIFF_VDIM), BF16),
            pltpu.VMEM((DIFF_HEADS * 2 * DEC_SEQ, 1), F32),
            pltpu.VMEM((DIFF_HEADS * 2 * DEC_SEQ, 1), F32),
            pltpu.VMEM((DIFF_HEADS * 2 * DEC_SEQ, DIFF_VDIM), F32),
        ],
    )
    return pl.pallas_call(
        functools.partial(_sample_attn_kernel, pp=pp, lam_init=lam_init),
        grid_spec=grid_spec,
        out_shape=jax.ShapeDtypeStruct((ROWS_SAMPLE, DIFF_WIDTH), F32),
        compiler_params=_cparams(("parallel", "arbitrary")),
        name="sample_attn",
    )(page_table, proj, proj, proj, *([cache_k] * pp), *([cache_v] * pp), lam, subln_g)


def _dn_kernel(xq_ref, xk_ref, xv_ref, z_ref, ba_ref, wq_ref, wk_ref, wv_ref, alog_ref, dt_ref, ng_ref,
               sinit_ref, cinit_ref, o_ref, sout_ref, cout_ref, xp_ref, s_ref, *, nv, hg):
    c = pl.program_id(2)
    last = c == pl.num_programs(2) - 1
    width = hg * DN_DK
    hg0 = pl.program_id(1) * hg
    x_refs = (xq_ref, xk_ref, xv_ref)
    w_refs = (wq_ref, wk_ref, wv_ref)

    @pl.when(c == 0)
    def _():
        s_ref[...] = sinit_ref[...]
        for p in range(3):
            xp_ref[p, 0:8, :] = cinit_ref[:, p * width:(p + 1) * width]

    ys = []
    for p in range(3):
        xp_ref[p, 8:8 + nv, :] = x_refs[p][...]
        if nv < CHUNK:
            xp_ref[p, 8 + nv:, :] = jnp.zeros((CHUNK - nv, width), F32)
        y = None
        for i in range(CONV_K):
            off = 8 - (CONV_K - 1) + i
            term = xp_ref[p, off:off + CHUNK, :] * w_refs[p][i:i + 1, :]
            y = term if y is None else y + term
        ys.append(y * jax.nn.sigmoid(y))

    @pl.when(last)
    def _():
        for p in range(3):
            cout_ref[:, p * width:(p + 1) * width] = xp_ref[p, nv:nv + 8, :]

    if nv == CHUNK:
        for p in range(3):
            xp_ref[p, 0:8, :] = xp_ref[p, CHUNK:CHUNK + 8, :]

    ba = ba_ref[...]
    if nv < CHUNK:
        ba = jnp.concatenate([ba, jnp.zeros((CHUNK - nv, LANES), F32)], axis=0)
    valid = lax.broadcasted_iota(jnp.int32, (CHUNK, LANES), 0) < nv
    beta_all = jnp.where(valid, jax.nn.sigmoid(ba), 0.0)
    ab = ba + dt_ref[...]
    softplus = jnp.maximum(ab, 0.0) + jnp.log1p(jnp.exp(-jnp.abs(ab)))
    g_all = jnp.where(valid, -jnp.exp(alog_ref[...]) * softplus, 0.0)
    ri = lax.broadcasted_iota(jnp.int32, (CHUNK, CHUNK), 0)
    ci = lax.broadcasted_iota(jnp.int32, (CHUNK, CHUNK), 1)
    causal = ri >= ci
    strict = ri > ci
    eye = (ri == ci).astype(F32)
    cum = jnp.dot(causal.astype(F32), g_all, preferred_element_type=F32, precision=lax.Precision.HIGHEST)
    cum_t = cum.T

    z = z_ref[...]
    if nv < CHUNK:
        z = jnp.concatenate([z, jnp.zeros((CHUNK - nv, width), F32)], axis=0)

    for j in range(hg):
        sl = slice(j * DN_DK, (j + 1) * DN_DK)
        lane = lax.broadcasted_iota(jnp.int32, (CHUNK, LANES), 1)
        head = hg0 + j
        beta = jnp.sum(jnp.where(lane == head, beta_all, 0.0), axis=1, keepdims=True)
        gc = jnp.sum(jnp.where(lane == DN_HEADS + head, cum, 0.0), axis=1, keepdims=True)
        sub = lax.broadcasted_iota(jnp.int32, (LANES, CHUNK), 0)
        gr = jnp.sum(jnp.where(sub == DN_HEADS + head, cum_t, 0.0), axis=0, keepdims=True)
        g_last = gc[CHUNK - 1:CHUNK, :]

        q = ys[0][:, sl]
        k = ys[1][:, sl]
        v = ys[2][:, sl]
        q = q * lax.rsqrt(jnp.sum(q * q, axis=-1, keepdims=True) + EPS) * (DN_DK ** -0.5)
        k = k * lax.rsqrt(jnp.sum(k * k, axis=-1, keepdims=True) + EPS)

        decay = jnp.exp(jnp.where(causal, gc - gr, NEG))
        e_g = jnp.exp(gc)
        tail = jnp.exp(g_last - gr)
        k_t = k.T
        s_old = s_ref[j]
        qk16 = jnp.concatenate([q, k], axis=0).astype(BF16)
        a1 = jnp.dot(qk16, k_t.astype(BF16), preferred_element_type=F32)
        a2 = jnp.dot(qk16, s_old.astype(BF16), preferred_element_type=F32)
        n_mat = beta * jnp.where(strict, a1[CHUNK:] * decay, 0.0)
        rhs = beta * (v - e_g * a2[CHUNK:])
        xpow = -n_mat
        t_inv = eye + xpow
        for _ in range(int(math.log2(CHUNK)) - 1):
            xb = xpow.astype(BF16)
            xpow = jnp.dot(xb, xb, preferred_element_type=F32)
            t_inv = t_inv + jnp.dot(t_inv.astype(BF16), xpow.astype(BF16), preferred_element_type=F32)
        u = jnp.dot(t_inv.astype(BF16), rhs.astype(BF16), preferred_element_type=F32)
        ub = u.astype(BF16)
        o = e_g * a2[:CHUNK] + jnp.dot((a1[:CHUNK] * decay).astype(BF16), ub, preferred_element_type=F32)
        s_ref[j] = jnp.exp(g_last) * s_old + jnp.dot((k_t * tail).astype(BF16), ub, preferred_element_type=F32)

        zj = z[:, sl]
        on = _rms(o, ng_ref[...]) * (zj * jax.nn.sigmoid(zj))
        o_ref[:, sl] = on[:nv].astype(o_ref.dtype)

    @pl.when(last)
    def _():
        sout_ref[...] = s_ref[...]


def _dn(proj, tail, conv_w, alog_row, dt_row, ng, s_init, c_init, l, *, nv, row0, nseq, nchunk, shared_init,
        out_dtype, hg=DN_HEADS):
    width = hg * DN_DK
    nhg = DN_HEADS // hg
    q0 = 3 * DIFF_WIDTH // width
    per = DN_WIDTH // width
    rb0 = row0 // nv
    row_map = lambda s, g, c: rb0 + s * nchunk + c
    init_map = (lambda s: 0) if shared_init else (lambda s: s)
    x_spec = lambda sec: pl.BlockSpec((nv, width), lambda s, g, c: (row_map(s, g, c), q0 + sec * per + g))
    w_spec = lambda sec: pl.BlockSpec((None, CONV_K, width), lambda s, g, c: (l, 0, sec * per + g))
    return pl.pallas_call(
        functools.partial(_dn_kernel, nv=nv, hg=hg),
        grid=(nseq, nhg, nchunk),
        in_specs=[
            x_spec(0), x_spec(1), x_spec(2), x_spec(3),
            pl.BlockSpec((nv, LANES), lambda s, g, c: (row_map(s, g, c), 0)),
            w_spec(0), w_spec(1), w_spec(2),
            pl.BlockSpec((None, 1, LANES), lambda s, g, c: (l, 0, 0)),
            pl.BlockSpec((None, 1, LANES), lambda s, g, c: (l, 0, 0)),
            pl.BlockSpec((None, 1, DN_DK), lambda s, g, c: (l, 0, 0)),
            pl.BlockSpec((None, hg, DN_DK, DN_DK), lambda s, g, c: (init_map(s), g, 0, 0)),
            pl.BlockSpec((None, 8, 3 * DN_WIDTH), lambda s, g, c: (init_map(s), 0, 0)),
        ],
        out_specs=[
            pl.BlockSpec((nv, width), lambda s, g, c: (s * nchunk + c, g)),
            pl.BlockSpec((None, hg, DN_DK, DN_DK), lambda s, g, c: (s, g, 0, 0)),
            pl.BlockSpec((None, 8, 3 * DN_WIDTH), lambda s, g, c: (s, 0, 0)),
        ],
        out_shape=[
            jax.ShapeDtypeStruct((nseq * nchunk * nv, DN_WIDTH), out_dtype),
            jax.ShapeDtypeStruct((nseq, DN_HEADS, DN_DK, DN_DK), F32),
            jax.ShapeDtypeStruct((nseq, 8, 3 * DN_WIDTH), F32),
        ],
        scratch_shapes=[
            pltpu.VMEM((3, CHUNK + 8, width), F32),
            pltpu.VMEM((hg, DN_DK, DN_DK), F32),
        ],
        compiler_params=_cparams(("parallel", "arbitrary", "arbitrary")),
        name=f"deltanet_{nv}",
    )(proj, proj, proj, proj, tail, conv_w, conv_w, conv_w, alog_row, dt_row, ng, s_init, c_init)


def kernel(x_prompt, x_sample, cache_k, cache_v, state_ssm, state_conv, page_table, meta_tokens, norm1_g, w_in,
           q_norm_g, k_norm_g, diff_lambda, subln_g, conv_w, a_log, dt_bias, dn_norm_g, w_out, norm2_g, w_up,
           w_down):
    n_pool = cache_k.shape[1]
    cache_k = cache_k.reshape(DEPTH, n_pool, PAGE_SIZE, DIFF_WIDTH)
    cache_v = cache_v.reshape(DEPTH, n_pool, PAGE_SIZE, DIFF_WIDTH)

    w_main = w_in[:, :, :MAIN_COLS].astype(BF16)
    w_tail = jnp.pad(w_in[:, :, MAIN_COLS:], ((0, 0), (0, 0), (0, LANES - 2 * DN_HEADS))).astype(BF16)
    w_out_b = w_out.astype(BF16)
    w_up_b = w_up.astype(BF16)
    w_down_b = w_down.astype(BF16)
    lane_pad = ((0, 0), (DN_HEADS, LANES - 2 * DN_HEADS))
    vec = lambda a: a.reshape(DEPTH, 1, a.shape[-1])
    alog_row = vec(jnp.pad(a_log, lane_pad))
    dt_row = vec(jnp.pad(dt_bias, lane_pad))
    norm1_g, norm2_g, q_norm_g, k_norm_g = vec(norm1_g), vec(norm2_g), vec(q_norm_g), vec(k_norm_g)
    subln_g, dn_norm_g = vec(subln_g), vec(dn_norm_g)
    conv_state = jnp.pad(state_conv, ((0, 0), (0, 0), (8 - (CONV_K - 1), 0), (0, 0)))

    x = jnp.concatenate([
        x_prompt.reshape(ROWS_PROMPT, D_MODEL),
        x_sample.reshape(ROWS_SAMPLE, D_MODEL),
        meta_tokens,
        jnp.zeros((N_ROWS - ROW_META0 - N_META, D_MODEL), F32),
    ], axis=0)

    zero_state = jnp.zeros((1, DN_HEADS, DN_DK, DN_DK), F32)
    zero_conv = jnp.zeros((1, 8, 3 * DN_WIDTH), F32)
    outs = [[] for _ in range(8)]
    for l in range(DEPTH):
        lam_init = 0.8 - 0.6 * math.exp(-0.3 * l)
        proj, tail = _in_proj(x, norm1_g, w_main, w_tail, q_norm_g, k_norm_g, l)
        k_all = proj[:, DIFF_WIDTH:2 * DIFF_WIDTH]
        v_all = proj[:, 2 * DIFF_WIDTH:3 * DIFF_WIDTH]
        meta = slice(ROW_META0, ROW_META0 + N_META)
        kmeta = jnp.pad(k_all[meta], ((0, LANES - N_META), (0, 0)))
        vmeta = jnp.pad(v_all[meta], ((0, LANES - N_META), (0, 0)))

        att_p = _prompt_attn(proj, kmeta, vmeta, diff_lambda, subln_g, l, lam_init)
        att_m = _meta_attn(proj[meta, :DIFF_WIDTH], kmeta, vmeta, diff_lambda, subln_g, l, lam_init)
        att_s = _sample_attn(page_table, proj, cache_k, cache_v, diff_lambda, subln_g, l, lam_init)

        dn_args = (proj, tail, conv_w, alog_row, dt_row, dn_norm_g)
        dn_m, s_meta, c_meta = _dn(*dn_args, zero_state, zero_conv, l, nv=N_META, row0=ROW_META0, nseq=1, nchunk=1,
                                   shared_init=True, out_dtype=F32)
        dn_s, s_samp, c_samp = _dn(*dn_args, state_ssm[l], conv_state[l], l, nv=DEC_SEQ, row0=ROW_SAMPLE0,
                                   nseq=DEC_BATCH, nchunk=1, shared_init=False, out_dtype=F32)
        dn_p, s_prom, c_prom = _dn(*dn_args, s_meta, c_meta, l, nv=CHUNK, row0=0, nseq=BATCH, nchunk=SEQ // CHUNK,
                                   shared_init=True, out_dtype=BF16)

        mix = jnp.concatenate([
            jnp.concatenate([att_p, dn_p], axis=1),
            jnp.concatenate([att_s, dn_s], axis=1).astype(BF16),
            jnp.concatenate([att_m, dn_m], axis=1).astype(BF16),
            jnp.zeros((N_ROWS - ROW_META0 - N_META, DIFF_WIDTH + DN_WIDTH), BF16),
        ], axis=0)
        x = _out_proj(mix, w_out_b, x, l)
        x = _ffn(x, norm2_g, w_up_b, w_down_b, l)

        def with_meta(t):
            real = t[:ROWS_PROMPT].reshape(BATCH, SEQ, DIFF_HEADS, DIFF_VDIM)
            m = jnp.broadcast_to(t[meta].reshape(1, N_META, DIFF_HEADS, DIFF_VDIM),
                                 (BATCH, N_META, DIFF_HEADS, DIFF_VDIM))
            return jnp.concatenate([m, real], axis=1)

        samp = slice(ROW_SAMPLE0, ROW_SAMPLE0 + ROWS_SAMPLE)
        outs[0].append(with_meta(k_all))
        outs[1].append(with_meta(v_all))
        outs[2].append(k_all[samp].reshape(DEC_BATCH, DEC_SEQ, DIFF_HEADS, DIFF_VDIM))
        outs[3].append(v_all[samp].reshape(DEC_BATCH, DEC_SEQ, DIFF_HEADS, DIFF_VDIM))
        outs[4].append(s_prom)
        outs[5].append(s_samp)
        outs[6].append(c_prom[:, 8 - (CONV_K - 1):])
        outs[7].append(c_samp[:, 8 - (CONV_K - 1):])

    y_prompt = x[:ROWS_PROMPT].reshape(BATCH, SEQ, D_MODEL)
    y_sample = x[ROW_SAMPLE0:ROW_SAMPLE0 + ROWS_SAMPLE].reshape(DEC_BATCH, DEC_SEQ, D_MODEL)
    return (y_prompt, y_sample) + tuple(jnp.stack(o) for o in outs)
```

```python
import functools
import math

import jax
import jax.numpy as jnp
from jax import lax
from jax.experimental import pallas as pl
from jax.experimental.pallas import tpu as pltpu

F32 = jnp.float32
BF16 = jnp.bfloat16

D_MODEL = 2048
BATCH = 4
SEQ = 2048
DEPTH = 4
DEC_BATCH = 8
DEC_SEQ = 8
PAGE_SIZE = 128
N_META = 16
HEAD_DIM = 128
DIFF_HEADS = 4
DIFF_VDIM = 2 * HEAD_DIM
DIFF_WIDTH = DIFF_HEADS * DIFF_VDIM
DN_HEADS = 8
DN_DK = 128
DN_WIDTH = DN_HEADS * DN_DK
CONV_K = 4
D_FF = 4 * D_MODEL
EPS = 1e-6
MAIN_COLS = 3 * DIFF_WIDTH + 4 * DN_WIDTH

ROWS_PROMPT = BATCH * SEQ
ROWS_SAMPLE = DEC_BATCH * DEC_SEQ
ROW_SAMPLE0 = ROWS_PROMPT
ROW_META0 = ROWS_PROMPT + ROWS_SAMPLE
N_ROWS = 8320
TM = 640
TM_IN = 1040
LANES = 128
CHUNK = 128
NEG = -1e30
PAGE_ROWS = PAGE_SIZE * 2 * DIFF_HEADS
NSUB = 2 * DIFF_HEADS
ROW_CHUNK = 32
VMEM_LIMIT = 56 * 1024 * 1024


def _cparams(sem):
    return pltpu.CompilerParams(dimension_semantics=sem, vmem_limit_bytes=VMEM_LIMIT)


def _rms(x, gain):
    return x * lax.rsqrt(jnp.mean(x * x, axis=-1, keepdims=True) + EPS) * gain


def _in_proj_kernel(x_ref, g_ref, w_ref, wt_ref, qg_ref, kg_ref, o_ref, ot_ref, xn_ref, *, n_q, n_k):
    n = pl.program_id(1)

    @pl.when(n == 0)
    def _():
        xn_ref[...] = _rms(x_ref[...], g_ref[...]).astype(BF16)
        ot_ref[...] = jnp.dot(xn_ref[...], wt_ref[...], preferred_element_type=F32)

    acc = jnp.dot(xn_ref[...], w_ref[...], preferred_element_type=F32)

    def head_norm(gain):
        cols = [_rms(acc[:, j * HEAD_DIM:(j + 1) * HEAD_DIM], gain) for j in range(acc.shape[1] // HEAD_DIM)]
        return jnp.concatenate(cols, axis=1)

    @pl.when(n < n_q)
    def _():
        o_ref[...] = head_norm(qg_ref[...] * (HEAD_DIM ** -0.5))

    @pl.when(jnp.logical_and(n >= n_q, n < n_q + n_k))
    def _():
        o_ref[...] = head_norm(kg_ref[...])

    @pl.when(n >= n_q + n_k)
    def _():
        o_ref[...] = acc


def _in_proj(x, g, w_main, w_tail, qg, kg, l, tn=512):
    n_q = DIFF_WIDTH // tn
    return pl.pallas_call(
        functools.partial(_in_proj_kernel, n_q=n_q, n_k=n_q),
        grid=(N_ROWS // TM_IN, MAIN_COLS // tn),
        in_specs=[
            pl.BlockSpec((TM_IN, D_MODEL), lambda m, n: (m, 0)),
            pl.BlockSpec((None, 1, D_MODEL), lambda m, n: (l, 0, 0)),
            pl.BlockSpec((None, D_MODEL, tn), lambda m, n: (l, 0, n)),
            pl.BlockSpec((None, D_MODEL, LANES), lambda m, n: (l, 0, 0)),
            pl.BlockSpec((None, 1, HEAD_DIM), lambda m, n: (l, 0, 0)),
            pl.BlockSpec((None, 1, HEAD_DIM), lambda m, n: (l, 0, 0)),
        ],
        out_specs=[
            pl.BlockSpec((TM_IN, tn), lambda m, n: (m, n)),
            pl.BlockSpec((TM_IN, LANES), lambda m, n: (m, 0)),
        ],
        out_shape=[
            jax.ShapeDtypeStruct((N_ROWS, MAIN_COLS), F32),
            jax.ShapeDtypeStruct((N_ROWS, LANES), F32),
        ],
        scratch_shapes=[pltpu.VMEM((TM_IN, D_MODEL), BF16)],
        compiler_params=_cparams(("parallel", "arbitrary")),
        name="in_proj",
    )(x, g, w_main, w_tail, qg, kg)


def _out_proj_kernel(mix_ref, w_ref, x_ref, o_ref):
    o_ref[...] = x_ref[...] + jnp.dot(mix_ref[...], w_ref[...], preferred_element_type=F32)


def _out_proj(mix, w_out, x, l):
    k = mix.shape[1]
    return pl.pallas_call(
        _out_proj_kernel,
        grid=(N_ROWS // TM,),
        in_specs=[
            pl.BlockSpec((TM, k), lambda m: (m, 0)),
            pl.BlockSpec((None, k, D_MODEL), lambda m: (l, 0, 0)),
            pl.BlockSpec((TM, D_MODEL), lambda m: (m, 0)),
        ],
        out_specs=pl.BlockSpec((TM, D_MODEL), lambda m: (m, 0)),
        out_shape=jax.ShapeDtypeStruct((N_ROWS, D_MODEL), F32),
        compiler_params=_cparams(("parallel",)),
        name="out_proj",
    )(mix, w_out, x)


def _ffn_kernel(x_ref, g_ref, wu_ref, wd_ref, o_ref, xn_ref, acc_ref):
    f = pl.program_id(1)

    @pl.when(f == 0)
    def _():
        xn_ref[...] = _rms(x_ref[...], g_ref[...]).astype(BF16)
        acc_ref[...] = jnp.zeros_like(acc_ref)

    h = jnp.maximum(jnp.dot(xn_ref[...], wu_ref[...], preferred_element_type=F32), 0.0)
    acc_ref[...] += jnp.dot((h * h).astype(BF16), wd_ref[...], preferred_element_type=F32)

    @pl.when(f == pl.num_programs(1) - 1)
    def _():
        o_ref[...] = x_ref[...] + acc_ref[...]


def _ffn(x, g, w_up, w_down, l, tf=512):
    return pl.pallas_call(
        _ffn_kernel,
        grid=(N_ROWS // TM, D_FF // tf),
        in_specs=[
            pl.BlockSpec((TM, D_MODEL), lambda m, f: (m, 0)),
            pl.BlockSpec((None, 1, D_MODEL), lambda m, f: (l, 0, 0)),
            pl.BlockSpec((None, D_MODEL, tf), lambda m, f: (l, 0, f)),
            pl.BlockSpec((None, tf, D_MODEL), lambda m, f: (l, f, 0)),
        ],
        out_specs=pl.BlockSpec((TM, D_MODEL), lambda m, f: (m, 0)),
        out_shape=jax.ShapeDtypeStruct((N_ROWS, D_MODEL), F32),
        scratch_shapes=[pltpu.VMEM((TM, D_MODEL), BF16), pltpu.VMEM((TM, D_MODEL), F32)],
        compiler_params=_cparams(("parallel", "arbitrary")),
        name="ffn",
    )(x, g, w_up, w_down)


def _diff_lambda(lam_ref, lam_init):
    lf = lam_ref[...]
    s1 = jnp.sum(lf[0:1] * lf[1:2], axis=-1, keepdims=True)
    s2 = jnp.sum(lf[2:3] * lf[3:4], axis=-1, keepdims=True)
    return jnp.exp(s1) - jnp.exp(s2) + lam_init


def _diff_finish(o1, o2, lam, g, lam_init):
    return _rms(o1 - lam * o2, g) * (1.0 - lam_init)


def _block_diag_q(q, t):
    z = jnp.zeros((t, HEAD_DIM), q.dtype)
    top = jnp.concatenate([q[:, :HEAD_DIM], z], axis=1)
    bot = jnp.concatenate([z, q[:, HEAD_DIM:]], axis=1)
    return jnp.concatenate([top, bot], axis=0)


_NT = (((1,), (1,)), ((), ()))


def _prompt_attn_kernel(q_ref, k_ref, v_ref, km_ref, vm_ref, lam_ref, g_ref, o_ref,
                        kb_ref, vb_ref, s_ref, p_ref, m_ref, l_ref, a_ref, acc_ref, *, tq, lam_init):
    qi = pl.program_id(2)
    rows = 2 * tq

    @pl.when(qi == 0)
    def _():
        kb_ref[...] = k_ref[...].astype(BF16)
        vb_ref[...] = v_ref[...].astype(BF16)

    q = q_ref[...].astype(BF16)
    q1 = q[:, :HEAD_DIM]
    q2 = q[:, HEAD_DIM:]

    def step(kblk, vblk, nk, visible, first):
        s_ref[0:tq, 0:nk] = lax.dot_general(q1, kblk[:, :HEAD_DIM], _NT, preferred_element_type=F32)
        s_ref[tq:rows, 0:nk] = lax.dot_general(q2, kblk[:, HEAD_DIM:], _NT, preferred_element_type=F32)
        reps = nk // LANES
        for rc in range(rows // ROW_CHUNK):
            rs = slice(rc * ROW_CHUNK, (rc + 1) * ROW_CHUNK)
            s = s_ref[rs, 0:nk]
            if visible is not None:
                r = lax.broadcasted_iota(jnp.int32, (ROW_CHUNK, nk), 0) + (rc * ROW_CHUNK) % tq
                c = lax.broadcasted_iota(jnp.int32, (ROW_CHUNK, nk), 1)
                s = jnp.where(visible(r, c), s, NEG)
            m_new = jnp.broadcast_to(jnp.max(s, axis=-1, keepdims=True), (ROW_CHUNK, LANES))
            if not first:
                m_prev = m_ref[rs, :]
                m_new = jnp.maximum(m_prev, m_new)
                alpha = jnp.exp(m_prev - m_new)
                a_ref[rs, :] = alpha
            m_ref[rs, :] = m_new
            p = jnp.exp(s - jnp.concatenate([m_new] * reps, axis=1))
            p_ref[rs, 0:nk] = p.astype(BF16)
            psum = p[:, 0:LANES]
            for i in range(1, reps):
                psum = psum + p[:, i * LANES:(i + 1) * LANES]
            l_ref[rs, :] = psum if first else alpha * l_ref[rs, :] + psum
        pv = jnp.dot(p_ref[:, 0:nk], vblk, preferred_element_type=F32)
        if first:
            acc_ref[...] = pv
        else:
            alpha = a_ref[...]
            acc_ref[...] = jnp.concatenate([alpha, alpha], axis=1) * acc_ref[...] + pv

    km = km_ref[...].astype(BF16)
    vm = vm_ref[...].astype(BF16)

    @pl.when(qi % 2 == 0)
    def _():
        off = pl.multiple_of(qi * tq, tq)
        step(jnp.concatenate([km, kb_ref[pl.ds(off, tq), :]], axis=0),
             jnp.concatenate([vm, vb_ref[pl.ds(off, tq), :]], axis=0), LANES + tq,
             lambda r, c: jnp.logical_or(c < N_META, jnp.logical_and(c >= LANES, c - LANES <= r)), True)

    @pl.when(qi % 2 == 1)
    def _():
        off = pl.multiple_of((qi - 1) * tq, tq)
        step(jnp.concatenate([km, kb_ref[pl.ds(off, 2 * tq), :]], axis=0),
             jnp.concatenate([vm, vb_ref[pl.ds(off, 2 * tq), :]], axis=0), LANES + 2 * tq,
             lambda r, c: jnp.logical_or(c < N_META, jnp.logical_and(c >= LANES, c - LANES <= r + tq)), True)

    def body(j, carry):
        off = pl.multiple_of(j * (2 * tq), 2 * tq)
        step(kb_ref[pl.ds(off, 2 * tq), :], vb_ref[pl.ds(off, 2 * tq), :], 2 * tq, None, False)
        return carry

    lax.fori_loop(0, qi // 2, body, 0)

    inv_l = 1.0 / jnp.sum(l_ref[...], axis=-1, keepdims=True)
    o = acc_ref[...] * inv_l
    lam = _diff_lambda(lam_ref, lam_init)
    o_ref[...] = _diff_finish(o[:tq], o[tq:], lam, g_ref[...], lam_init).astype(o_ref.dtype)


def _prompt_attn(proj, kmeta, vmeta, lam, subln_g, l, lam_init, tq=256):
    nq = SEQ // tq
    kcol0 = DIFF_WIDTH // DIFF_VDIM
    return pl.pallas_call(
        functools.partial(_prompt_attn_kernel, tq=tq, lam_init=lam_init),
        grid=(BATCH, DIFF_HEADS, nq),
        in_specs=[
            pl.BlockSpec((tq, DIFF_VDIM), lambda b, h, i: (b * nq + i, h)),
            pl.BlockSpec((SEQ, DIFF_VDIM), lambda b, h, i: (b, kcol0 + h)),
            pl.BlockSpec((SEQ, DIFF_VDIM), lambda b, h, i: (b, 2 * kcol0 + h)),
            pl.BlockSpec((LANES, DIFF_VDIM), lambda b, h, i: (0, h)),
            pl.BlockSpec((LANES, DIFF_VDIM), lambda b, h, i: (0, h)),
            pl.BlockSpec((None, 4, HEAD_DIM), lambda b, h, i: (l, 0, 0)),
            pl.BlockSpec((None, 1, DIFF_VDIM), lambda b, h, i: (l, 0, 0)),
        ],
        out_specs=pl.BlockSpec((tq, DIFF_VDIM), lambda b, h, i: (b * nq + i, h)),
        out_shape=jax.ShapeDtypeStruct((ROWS_PROMPT, DIFF_WIDTH), BF16),
        scratch_shapes=[
            pltpu.VMEM((SEQ, DIFF_VDIM), BF16),
            pltpu.VMEM((SEQ, DIFF_VDIM), BF16),
            pltpu.VMEM((2 * tq, 2 * tq + LANES), F32),
            pltpu.VMEM((2 * tq, 2 * tq + LANES), BF16),
            pltpu.VMEM((2 * tq, LANES), F32),
            pltpu.VMEM((2 * tq, LANES), F32),
            pltpu.VMEM((2 * tq, LANES), F32),
            pltpu.VMEM((2 * tq, DIFF_VDIM), F32),
        ],
        compiler_params=_cparams(("parallel", "parallel", "arbitrary")),
        name="prompt_attn",
    )(proj, proj, proj, kmeta, vmeta, lam, subln_g)


def _meta_attn_kernel(q_ref, km_ref, vm_ref, lam_ref, g_ref, o_ref, *, lam_init):
    t = N_META
    lam = _diff_lambda(lam_ref, lam_init)
    row = lax.broadcasted_iota(jnp.int32, (2 * t, LANES), 0)
    row = jnp.where(row >= t, row - t, row)
    col = lax.broadcasted_iota(jnp.int32, (2 * t, LANES), 1)
    for h in range(DIFF_HEADS):
        sl = slice(h * DIFF_VDIM, (h + 1) * DIFF_VDIM)
        qbd = _block_diag_q(q_ref[:, sl], t).astype(BF16)
        s = lax.dot_general(qbd, km_ref[:, sl].astype(BF16), _NT, preferred_element_type=F32)
        s = jnp.where(col <= row, s, NEG)
        p = jnp.exp(s - jnp.max(s, axis=-1, keepdims=True))
        o = jnp.dot(p.astype(BF16), vm_ref[:, sl].astype(BF16), preferred_element_type=F32)
        o = o / jnp.sum(p, axis=-1, keepdims=True)
        o_ref[:, sl] = _diff_finish(o[:t], o[t:], lam, g_ref[...], lam_init)


def _meta_attn(qmeta, kmeta, vmeta, lam, subln_g, l, lam_init):
    return pl.pallas_call(
        functools.partial(_meta_attn_kernel, lam_init=lam_init),
        grid=(1,),
        in_specs=[
            pl.BlockSpec((N_META, DIFF_WIDTH), lambda i: (0, 0)),
            pl.BlockSpec((LANES, DIFF_WIDTH), lambda i: (0, 0)),
            pl.BlockSpec((LANES, DIFF_WIDTH), lambda i: (0, 0)),
            pl.BlockSpec((None, 4, HEAD_DIM), lambda i: (l, 0, 0)),
            pl.BlockSpec((None, 1, DIFF_VDIM), lambda i: (l, 0, 0)),
        ],
        out_specs=pl.BlockSpec((N_META, DIFF_WIDTH), lambda i: (0, 0)),
        out_shape=jax.ShapeDtypeStruct((N_META, DIFF_WIDTH), F32),
        compiler_params=_cparams(("arbitrary",)),
        name="meta_attn",
    )(qmeta, kmeta, vmeta, lam, subln_g)


def _by_subhead(x):
    return jnp.concatenate([x[:, h * DIFF_VDIM + c * HEAD_DIM:h * DIFF_VDIM + (c + 1) * HEAD_DIM]
                            for c in range(2) for h in range(DIFF_HEADS)], axis=0)


def _sample_attn_kernel(pt_ref, q_ref, kn_ref, vn_ref, *rest, pp, lam_init):
    kp_refs = rest[:pp]
    vp_refs = rest[pp:2 * pp]
    lam_ref, g_ref, o_ref, qs_ref, m_ref, l_ref, acc_ref = rest[2 * pp:]
    t = DEC_SEQ
    nq = NSUB * t
    g = pl.program_id(1)

    @pl.when(g == 0)
    def _():
        qs_ref[...] = _by_subhead(q_ref[...]).astype(BF16)
        m_ref[...] = jnp.full_like(m_ref, NEG)
        l_ref[...] = jnp.zeros_like(l_ref)
        acc_ref[...] = jnp.zeros_like(acc_ref)

    def attend(k_blocks, v_blocks, bias, dist):
        s = []
        for kb in k_blocks:
            sk = lax.dot_general(qs_ref[...], kb, _NT, preferred_element_type=F32)
            s.append([sk[:, i * LANES:(i + 1) * LANES] + bias for i in range(sk.shape[1] // LANES)])
        m_prev = m_ref[...]
        m_tile = None
        for sk in s:
            for st in sk:
                m_tile = st if m_tile is None else jnp.maximum(m_tile, st)
        m_new = jnp.maximum(m_prev, jnp.max(m_tile, axis=-1, keepdims=True))
        alpha = jnp.exp(m_prev - m_new)
        lane = lax.broadcasted_iota(jnp.int32, (nq, LANES), 1)
        low = (lane & dist) == 0
        l_tile = None
        pv = None
        for sk, vb in zip(s, v_blocks):
            p2 = []
            for st in sk:
                p = jnp.exp(st - m_new)
                l_tile = p if l_tile is None else l_tile + p
                sw = jnp.where(low, pltpu.roll(p, LANES - dist, 1), pltpu.roll(p, dist, 1))
                p2.append(jnp.concatenate([p, sw], axis=0).astype(BF16))
            d = jnp.dot(jnp.concatenate(p2, axis=1), vb, preferred_element_type=F32)
            pv = d if pv is None else pv + d
        acc_ref[...] = jnp.concatenate([alpha, alpha], axis=0) * acc_ref[...] + pv
        l_ref[...] = alpha * l_ref[...] + jnp.sum(l_tile, axis=-1, keepdims=True)
        m_ref[...] = m_new

    row = lax.broadcasted_iota(jnp.int32, (nq, LANES), 0)
    col = lax.broadcasted_iota(jnp.int32, (nq, LANES), 1)
    bias = jnp.where((col & (NSUB - 1)) == row // t, 0.0, NEG)
    attend([r[...].astype(BF16) for r in kp_refs], [r[...].astype(BF16) for r in vp_refs], bias, DIFF_HEADS)

    @pl.when(g == pl.num_programs(1) - 1)
    def _():
        zpad = jnp.zeros((LANES - nq, HEAD_DIM), F32)
        kn = jnp.concatenate([_by_subhead(kn_ref[...]), zpad], axis=0).astype(BF16)
        vn = jnp.concatenate([_by_subhead(vn_ref[...]), zpad], axis=0).astype(BF16)
        r = lax.broadcasted_iota(jnp.int32, (nq, LANES), 0)
        c = lax.broadcasted_iota(jnp.int32, (nq, LANES), 1)
        valid = jnp.logical_and(c // t == r // t, c % t <= r % t)
        attend([kn], [vn], jnp.where(valid, 0.0, NEG), DIFF_HEADS * t)
        lam = _diff_lambda(lam_ref, lam_init)
        o = acc_ref[...] / jnp.concatenate([l_ref[...], l_ref[...]], axis=0)
        for h in range(DIFF_HEADS):
            r1 = h * t
            r2 = (DIFF_HEADS + h) * t
            o1 = jnp.concatenate([o[r1:r1 + t], o[nq + r1:nq + r1 + t]], axis=1)
            o2 = jnp.concatenate([o[nq + r2:nq + r2 + t], o[r2:r2 + t]], axis=1)
            o_ref[:, h * DIFF_VDIM:(h + 1) * DIFF_VDIM] = _diff_finish(o1, o2, lam, g_ref[...], lam_init)


def _sample_attn(page_table, proj, cache_k, cache_v, lam, subln_g, l, lam_init, pp=8):
    n_pages = page_table.shape[1]
    row0 = ROW_SAMPLE0 // DEC_SEQ
    nq = NSUB * DEC_SEQ
    qkv_spec = lambda c: pl.BlockSpec((DEC_SEQ, DIFF_WIDTH), lambda b, g, pt: (row0 + b, c))
    page_spec = lambda i: pl.BlockSpec((None, None, PAGE_ROWS, HEAD_DIM),
                                       lambda b, g, pt: (l, pt[b, g * pp + i], 0, 0))
    grid_spec = pltpu.PrefetchScalarGridSpec(
        num_scalar_prefetch=1,
        grid=(DEC_BATCH, n_pages // pp),
        in_specs=[qkv_spec(0), qkv_spec(1), qkv_spec(2)]
        + [page_spec(i) for i in range(pp)] + [page_spec(i) for i in range(pp)]
        + [pl.BlockSpec((None, 4, HEAD_DIM), lambda b, g, pt: (l, 0, 0)),
           pl.BlockSpec((None, 1, DIFF_VDIM), lambda b, g, pt: (l, 0, 0))],
        out_specs=pl.BlockSpec((DEC_SEQ, DIFF_WIDTH), lambda b, g, pt: (b, 0)),
        scratch_shapes=[
            pltpu.VMEM((nq, HEAD_DIM), BF16),
            pltpu.VMEM((nq, 1), F32),
            pltpu.VMEM((nq, 1), F32),
            pltpu.VMEM((2 * nq, HEAD_DIM), F32),
        ],
    )
    return pl.pallas_call(
        functools.partial(_sample_attn_kernel, pp=pp, lam_init=lam_init),
        grid_spec=grid_spec,
        out_shape=jax.ShapeDtypeStruct((ROWS_SAMPLE, DIFF_WIDTH), F32),
        compiler_params=_cparams(("parallel", "arbitrary")),
        name="sample_attn",
    )(page_table, proj, proj, proj, *([cache_k] * pp), *([cache_v] * pp), lam, subln_g)


def _page_view(cache):
    d, n = cache.shape[:2]
    c = cache.reshape(d, n, PAGE_SIZE, DIFF_HEADS, 2, HEAD_DIM)
    return jnp.transpose(c, (0, 1, 2, 4, 3, 5)).reshape(d, n, PAGE_ROWS, HEAD_DIM)


def _dn_kernel(xq_ref, xk_ref, xv_ref, z_ref, ba_ref, wq_ref, wk_ref, wv_ref, alog_ref, dt_ref, ng_ref,
               sinit_ref, cinit_ref, o_ref, sout_ref, cout_ref, xp_ref, s_ref, *, nv, hg):
    c = pl.program_id(2)
    last = c == pl.num_programs(2) - 1
    width = hg * DN_DK
    x_refs = (xq_ref, xk_ref, xv_ref)
    w_refs = (wq_ref, wk_ref, wv_ref)
    heads = range(hg)

    @pl.when(c == 0)
    def _():
        s_ref[...] = sinit_ref[...]
        for p in range(3):
            xp_ref[p, 0:8, :] = cinit_ref[:, p * width:(p + 1) * width]

    ys = []
    for p in range(3):
        xp_ref[p, 8:8 + nv, :] = x_refs[p][...]
        if nv < CHUNK:
            xp_ref[p, 8 + nv:, :] = jnp.zeros((CHUNK - nv, width), F32)
        xpv = xp_ref[p]
        y = xpv[8:, :] * w_refs[p][CONV_K - 1:CONV_K, :]
        for d in range(1, CONV_K):
            y = y + pltpu.roll(xpv, d, 0)[8:, :] * w_refs[p][CONV_K - 1 - d:CONV_K - d, :]
        ys.append(y * jax.nn.sigmoid(y))

    @pl.when(last)
    def _():
        for p in range(3):
            cout_ref[:, p * width:(p + 1) * width] = xp_ref[p, nv:nv + 8, :]

    if nv == CHUNK:
        for p in range(3):
            xp_ref[p, 0:8, :] = xp_ref[p, CHUNK:CHUNK + 8, :]

    ba = ba_ref[...]
    if nv < CHUNK:
        ba = jnp.concatenate([ba, jnp.zeros((CHUNK - nv, LANES), F32)], axis=0)
    gates = ba.T[0:2 * DN_HEADS, :]
    valid = lax.broadcasted_iota(jnp.int32, gates.shape, 1) < nv
    is_beta = lax.broadcasted_iota(jnp.int32, gates.shape, 0) < DN_HEADS
    ab = gates + dt_ref[...]
    softplus = jnp.maximum(ab, 0.0) + jnp.log1p(jnp.exp(-jnp.abs(ab)))
    g_t = jnp.where(jnp.logical_and(valid, jnp.logical_not(is_beta)), -jnp.exp(alog_ref[...]) * softplus, 0.0)
    ri = lax.broadcasted_iota(jnp.int32, (CHUNK, CHUNK), 0)
    ci = lax.broadcasted_iota(jnp.int32, (CHUNK, CHUNK), 1)
    causal = ri >= ci
    eye = (ri == ci).astype(F32)
    cum_t = jnp.dot(g_t, (ri <= ci).astype(F32), preferred_element_type=F32, precision=lax.Precision.HIGHEST)
    slab = jnp.where(is_beta, jnp.where(valid, jax.nn.sigmoid(gates), 0.0), cum_t)
    cols = jnp.concatenate([slab, jnp.zeros((CHUNK - 2 * DN_HEADS, CHUNK), F32)], axis=0).T

    z = z_ref[...]
    if nv < CHUNK:
        z = jnp.concatenate([z, jnp.zeros((CHUNK - nv, width), F32)], axis=0)

    sl = [slice(j * DN_DK, (j + 1) * DN_DK) for j in heads]
    beta = [cols[:, j:j + 1] for j in heads]
    gc = [cols[:, DN_HEADS + j:DN_HEADS + j + 1] for j in heads]
    gr = [cum_t[DN_HEADS + j:DN_HEADS + j + 1, :] for j in heads]
    g_last = [gc[j][CHUNK - 1:CHUNK, :] for j in heads]

    def l2n(x, scale):
        return x * (lax.rsqrt(jnp.sum(x * x, axis=-1, keepdims=True) + EPS) * scale)

    q = [l2n(ys[0][:, sl[j]], DN_DK ** -0.5) for j in heads]
    k = [l2n(ys[1][:, sl[j]], 1.0) for j in heads]
    v = [ys[2][:, sl[j]] for j in heads]
    k_t = [k[j].T for j in heads]
    s_old = [s_ref[j] for j in heads]
    qk16 = [jnp.concatenate([q[j], k[j]], axis=0).astype(BF16) for j in heads]
    a1 = [jnp.dot(qk16[j], k_t[j].astype(BF16), preferred_element_type=F32) for j in heads]
    a2 = [jnp.dot(qk16[j], s_old[j].astype(BF16), preferred_element_type=F32) for j in heads]
    decay = [jnp.exp(jnp.where(causal, gc[j] - gr[j], NEG)) for j in heads]
    e_g = [jnp.exp(gc[j]) for j in heads]
    rhs = [(beta[j] * (v[j] - e_g[j] * a2[j][CHUNK:])).astype(BF16) for j in heads]
    xpow = [a1[j][CHUNK:] * ((decay[j] - eye) * (-beta[j])) for j in heads]
    t_inv = [eye + xpow[j] for j in heads]
    xb = [xpow[j].astype(BF16) for j in heads]
    for _ in range(int(math.log2(CHUNK)) - 1):
        xb = [jnp.dot(xb[j], xb[j], preferred_element_type=F32).astype(BF16) for j in heads]
        t_inv = [t_inv[j] + jnp.dot(t_inv[j].astype(BF16), xb[j], preferred_element_type=F32) for j in heads]
    u = [jnp.dot(t_inv[j].astype(BF16), rhs[j], preferred_element_type=F32).astype(BF16) for j in heads]
    o = [e_g[j] * a2[j][:CHUNK] + jnp.dot((a1[j][:CHUNK] * decay[j]).astype(BF16), u[j],
                                          preferred_element_type=F32) for j in heads]
    for j in heads:
        tail = jnp.exp(g_last[j] - gr[j])
        s_ref[j] = jnp.exp(g_last[j]) * s_old[j] + jnp.dot((k_t[j] * tail).astype(BF16), u[j],
                                                           preferred_element_type=F32)
    for j in heads:
        zj = z[:, sl[j]]
        on = _rms(o[j], ng_ref[...]) * (zj * jax.nn.sigmoid(zj))
        o_ref[:, sl[j]] = on[:nv].astype(o_ref.dtype)

    @pl.when(last)
    def _():
        sout_ref[...] = s_ref[...]


def _dn(proj, tail, conv_w, alog_rows, dt_rows, ng, s_init, c_init, l, *, nv, row0, nseq, nchunk, shared_init,
        out_dtype):
    hg = DN_HEADS
    width = hg * DN_DK
    q0 = 3 * DIFF_WIDTH // width
    rb0 = row0 // nv
    row_map = lambda s, g, c: rb0 + s * nchunk + c
    init_map = (lambda s: 0) if shared_init else (lambda s: s)
    x_spec = lambda sec: pl.BlockSpec((nv, width), lambda s, g, c: (row_map(s, g, c), q0 + sec))
    w_spec = lambda sec: pl.BlockSpec((None, CONV_K, width), lambda s, g, c: (l, 0, sec))
    return pl.pallas_call(
        functools.partial(_dn_kernel, nv=nv, hg=hg),
        grid=(nseq, 1, nchunk),
        in_specs=[
            x_spec(0), x_spec(1), x_spec(2), x_spec(3),
            pl.BlockSpec((nv, LANES), lambda s, g, c: (row_map(s, g, c), 0)),
            w_spec(0), w_spec(1), w_spec(2),
            pl.BlockSpec((None, 2 * DN_HEADS, LANES), lambda s, g, c: (l, 0, 0)),
            pl.BlockSpec((None, 2 * DN_HEADS, LANES), lambda s, g, c: (l, 0, 0)),
            pl.BlockSpec((None, 1, DN_DK), lambda s, g, c: (l, 0, 0)),
            pl.BlockSpec((None, hg, DN_DK, DN_DK), lambda s, g, c: (init_map(s), 0, 0, 0)),
            pl.BlockSpec((None, 8, 3 * DN_WIDTH), lambda s, g, c: (init_map(s), 0, 0)),
        ],
        out_specs=[
            pl.BlockSpec((nv, width), lambda s, g, c: (s * nchunk + c, 0)),
            pl.BlockSpec((None, hg, DN_DK, DN_DK), lambda s, g, c: (s, 0, 0, 0)),
            pl.BlockSpec((None, 8, 3 * DN_WIDTH), lambda s, g, c: (s, 0, 0)),
        ],
        out_shape=[
            jax.ShapeDtypeStruct((nseq * nchunk * nv, DN_WIDTH), out_dtype),
            jax.ShapeDtypeStruct((nseq, DN_HEADS, DN_DK, DN_DK), F32),
            jax.ShapeDtypeStruct((nseq, 8, 3 * DN_WIDTH), F32),
        ],
        scratch_shapes=[
            pltpu.VMEM((3, CHUNK + 8, width), F32),
            pltpu.VMEM((hg, DN_DK, DN_DK), F32),
        ],
        compiler_params=_cparams(("parallel", "arbitrary", "arbitrary")),
        name=f"deltanet_{nv}",
    )(proj, proj, proj, proj, tail, conv_w, conv_w, conv_w, alog_rows, dt_rows, ng, s_init, c_init)


def kernel(x_prompt, x_sample, cache_k, cache_v, state_ssm, state_conv, page_table, meta_tokens, norm1_g, w_in,
           q_norm_g, k_norm_g, diff_lambda, subln_g, conv_w, a_log, dt_bias, dn_norm_g, w_out, norm2_g, w_up,
           w_down):
    cache_k = _page_view(cache_k)
    cache_v = _page_view(cache_v)

    w_main = w_in[:, :, :MAIN_COLS].astype(BF16)
    w_tail = jnp.pad(w_in[:, :, MAIN_COLS:], ((0, 0), (0, 0), (0, LANES - 2 * DN_HEADS))).astype(BF16)
    w_out_b = w_out.astype(BF16)
    w_up_b = w_up.astype(BF16)
    w_down_b = w_down.astype(BF16)
    vec = lambda a: a.reshape(DEPTH, 1, a.shape[-1])
    gate_rows = lambda a: jnp.broadcast_to(jnp.pad(a, ((0, 0), (DN_HEADS, 0)))[:, :, None],
                                           (DEPTH, 2 * DN_HEADS, LANES))
    alog_rows = gate_rows(a_log)
    dt_rows = gate_rows(dt_bias)
    norm1_g, norm2_g, q_norm_g, k_norm_g = vec(norm1_g), vec(norm2_g), vec(q_norm_g), vec(k_norm_g)
    subln_g, dn_norm_g = vec(subln_g), vec(dn_norm_g)
    conv_state = jnp.pad(state_conv, ((0, 0), (0, 0), (8 - (CONV_K - 1), 0), (0, 0)))

    x = jnp.concatenate([
        x_prompt.reshape(ROWS_PROMPT, D_MODEL),
        x_sample.reshape(ROWS_SAMPLE, D_MODEL),
        meta_tokens,
        jnp.zeros((N_ROWS - ROW_META0 - N_META, D_MODEL), F32),
    ], axis=0)

    zero_state = jnp.zeros((1, DN_HEADS, DN_DK, DN_DK), F32)
    zero_conv = jnp.zeros((1, 8, 3 * DN_WIDTH), F32)
    outs = [[] for _ in range(8)]
    for l in range(DEPTH):
        lam_init = 0.8 - 0.6 * math.exp(-0.3 * l)
        proj, tail = _in_proj(x, norm1_g, w_main, w_tail, q_norm_g, k_norm_g, l)
        k_all = proj[:, DIFF_WIDTH:2 * DIFF_WIDTH]
        v_all = proj[:, 2 * DIFF_WIDTH:3 * DIFF_WIDTH]
        meta = slice(ROW_META0, ROW_META0 + N_META)
        kmeta = jnp.pad(k_all[meta], ((0, LANES - N_META), (0, 0)))
        vmeta = jnp.pad(v_all[meta], ((0, LANES - N_META), (0, 0)))

        att_p = _prompt_attn(proj, kmeta, vmeta, diff_lambda, subln_g, l, lam_init)
        att_m = _meta_attn(proj[meta, :DIFF_WIDTH], kmeta, vmeta, diff_lambda, subln_g, l, lam_init)
        att_s = _sample_attn(page_table, proj, cache_k, cache_v, diff_lambda, subln_g, l, lam_init)

        dn_args = (proj, tail, conv_w, alog_rows, dt_rows, dn_norm_g)
        dn_m, s_meta, c_meta = _dn(*dn_args, zero_state, zero_conv, l, nv=N_META, row0=ROW_META0, nseq=1, nchunk=1,
                                   shared_init=True, out_dtype=F32)
        dn_s, s_samp, c_samp = _dn(*dn_args, state_ssm[l], conv_state[l], l, nv=DEC_SEQ, row0=ROW_SAMPLE0,
                                   nseq=DEC_BATCH, nchunk=1, shared_init=False, out_dtype=F32)
        dn_p, s_prom, c_prom = _dn(*dn_args, s_meta, c_meta, l, nv=CHUNK, row0=0, nseq=BATCH, nchunk=SEQ // CHUNK,
                                   shared_init=True, out_dtype=BF16)

        mix = jnp.concatenate([
            jnp.concatenate([att_p, dn_p], axis=1),
            jnp.concatenate([att_s, dn_s], axis=1).astype(BF16),
            jnp.concatenate([att_m, dn_m], axis=1).astype(BF16),
            jnp.zeros((N_ROWS - ROW_META0 - N_META, DIFF_WIDTH + DN_WIDTH), BF16),
        ], axis=0)
        x = _out_proj(mix, w_out_b, x, l)
        x = _ffn(x, norm2_g, w_up_b, w_down_b, l)

        def with_meta(t):
            real = t[:ROWS_PROMPT].reshape(BATCH, SEQ, DIFF_HEADS, DIFF_VDIM)
            m = jnp.broadcast_to(t[meta].reshape(1, N_META, DIFF_HEADS, DIFF_VDIM),
                                 (BATCH, N_META, DIFF_HEADS, DIFF_VDIM))
            return jnp.concatenate([m, real], axis=1)

        samp = slice(ROW_SAMPLE0, ROW_SAMPLE0 + ROWS_SAMPLE)
        outs[0].append(with_meta(k_all))
        outs[1].append(with_meta(v_all))
        outs[2].append(k_all[samp].reshape(DEC_BATCH, DEC_SEQ, DIFF_HEADS, DIFF_VDIM))
        outs[3].append(v_all[samp].reshape(DEC_BATCH, DEC_SEQ, DIFF_HEADS, DIFF_VDIM))
        outs[4].append(s_prom)
        outs[5].append(s_samp)
        outs[6].append(c_prom[:, 8 - (CONV_K - 1):])
        outs[7].append(c_samp[:, 8 - (CONV_K - 1):])

    y_prompt = x[:ROWS_PROMPT].reshape(BATCH, SEQ, D_MODEL)
    y_sample = x[ROW_SAMPLE0:ROW_SAMPLE0 + ROWS_SAMPLE].reshape(DEC_BATCH, DEC_SEQ, D_MODEL)
    return (y_prompt, y_sample) + tuple(jnp.stack(o) for o in outs)
```

```python
import functools
import math

import jax
import jax.numpy as jnp
from jax import lax
from jax.experimental import pallas as pl
from jax.experimental.pallas import tpu as pltpu

F32 = jnp.float32
BF16 = jnp.bfloat16

D_MODEL = 2048
BATCH = 4
SEQ = 2048
DEPTH = 4
DEC_BATCH = 8
DEC_SEQ = 8
PAGE_SIZE = 128
N_META = 16
HEAD_DIM = 128
DIFF_HEADS = 4
DIFF_VDIM = 2 * HEAD_DIM
DIFF_WIDTH = DIFF_HEADS * DIFF_VDIM
DN_HEADS = 8
DN_DK = 128
DN_WIDTH = DN_HEADS * DN_DK
CONV_K = 4
D_FF = 4 * D_MODEL
EPS = 1e-6
MAIN_COLS = 3 * DIFF_WIDTH + 4 * DN_WIDTH

ROWS_PROMPT = BATCH * SEQ
ROWS_SAMPLE = DEC_BATCH * DEC_SEQ
ROW_SAMPLE0 = ROWS_PROMPT
ROW_META0 = ROWS_PROMPT + ROWS_SAMPLE
N_ROWS = 8320
TM = 640
TM_IN = 1040
LANES = 128
CHUNK = 128
NEG = -1e30
PAGE_ROWS = PAGE_SIZE * 2 * DIFF_HEADS
NSUB = 2 * DIFF_HEADS
ROW_CHUNK = 32
VMEM_LIMIT = 56 * 1024 * 1024


def _cparams(sem):
    return pltpu.CompilerParams(dimension_semantics=sem, vmem_limit_bytes=VMEM_LIMIT)


def _rms(x, gain):
    return x * lax.rsqrt(jnp.mean(x * x, axis=-1, keepdims=True) + EPS) * gain


def _store_paged(dst_ref, cols, tile):
    n_rows = cols.shape[0]
    for j in range(cols.shape[1] // HEAD_DIM):
        piece = tile * (cols.shape[1] // HEAD_DIM) + j
        dst_ref[pl.ds((piece % 2) * DIFF_HEADS + piece // 2, n_rows, stride=NSUB), :] = (
            cols[:, j * HEAD_DIM:(j + 1) * HEAD_DIM])


def _in_proj_kernel(x_ref, g_ref, w_ref, wt_ref, qg_ref, kg_ref, o_ref, ot_ref, k2_ref, v2_ref, xn_ref, *,
                    n_q, n_k):
    n = pl.program_id(1)

    @pl.when(n == 0)
    def _():
        xn_ref[...] = _rms(x_ref[...], g_ref[...]).astype(BF16)
        ot_ref[...] = jnp.dot(xn_ref[...], wt_ref[...], preferred_element_type=F32)

    acc = jnp.dot(xn_ref[...], w_ref[...], preferred_element_type=F32)

    def head_norm(gain):
        cols = [_rms(acc[:, j * HEAD_DIM:(j + 1) * HEAD_DIM], gain) for j in range(acc.shape[1] // HEAD_DIM)]
        return jnp.concatenate(cols, axis=1)

    @pl.when(n < n_q)
    def _():
        o_ref[...] = head_norm(qg_ref[...] * (HEAD_DIM ** -0.5))

    for t in range(n_k):
        @pl.when(n == n_q + t)
        def _():
            kn = head_norm(kg_ref[...])
            o_ref[...] = kn
            _store_paged(k2_ref, kn, t)

        @pl.when(n == n_q + n_k + t)
        def _():
            o_ref[...] = acc
            _store_paged(v2_ref, acc, t)

    @pl.when(n >= n_q + 2 * n_k)
    def _():
        o_ref[...] = acc


def _in_proj(x, g, w_main, w_tail, qg, kg, l, tn=512):
    n_q = DIFF_WIDTH // tn
    return pl.pallas_call(
        functools.partial(_in_proj_kernel, n_q=n_q, n_k=n_q),
        grid=(N_ROWS // TM_IN, MAIN_COLS // tn),
        in_specs=[
            pl.BlockSpec((TM_IN, D_MODEL), lambda m, n: (m, 0)),
            pl.BlockSpec((None, 1, D_MODEL), lambda m, n: (l, 0, 0)),
            pl.BlockSpec((None, D_MODEL, tn), lambda m, n: (l, 0, n)),
            pl.BlockSpec((None, D_MODEL, LANES), lambda m, n: (l, 0, 0)),
            pl.BlockSpec((None, 1, HEAD_DIM), lambda m, n: (l, 0, 0)),
            pl.BlockSpec((None, 1, HEAD_DIM), lambda m, n: (l, 0, 0)),
        ],
        out_specs=[
            pl.BlockSpec((TM_IN, tn), lambda m, n: (m, n)),
            pl.BlockSpec((TM_IN, LANES), lambda m, n: (m, 0)),
            pl.BlockSpec((TM_IN * NSUB, HEAD_DIM), lambda m, n: (m, 0)),
            pl.BlockSpec((TM_IN * NSUB, HEAD_DIM), lambda m, n: (m, 0)),
        ],
        out_shape=[
            jax.ShapeDtypeStruct((N_ROWS, MAIN_COLS), F32),
            jax.ShapeDtypeStruct((N_ROWS, LANES), F32),
            jax.ShapeDtypeStruct((N_ROWS * NSUB, HEAD_DIM), F32),
            jax.ShapeDtypeStruct((N_ROWS * NSUB, HEAD_DIM), F32),
        ],
        scratch_shapes=[pltpu.VMEM((TM_IN, D_MODEL), BF16)],
        compiler_params=_cparams(("parallel", "arbitrary")),
        name="in_proj",
    )(x, g, w_main, w_tail, qg, kg)


def _out_proj_kernel(mix_ref, w_ref, x_ref, o_ref):
    o_ref[...] = x_ref[...] + jnp.dot(mix_ref[...], w_ref[...], preferred_element_type=F32)


def _out_proj(mix, w_out, x, l):
    k = mix.shape[1]
    return pl.pallas_call(
        _out_proj_kernel,
        grid=(N_ROWS // TM,),
        in_specs=[
            pl.BlockSpec((TM, k), lambda m: (m, 0)),
            pl.BlockSpec((None, k, D_MODEL), lambda m: (l, 0, 0)),
            pl.BlockSpec((TM, D_MODEL), lambda m: (m, 0)),
        ],
        out_specs=pl.BlockSpec((TM, D_MODEL), lambda m: (m, 0)),
        out_shape=jax.ShapeDtypeStruct((N_ROWS, D_MODEL), F32),
        compiler_params=_cparams(("parallel",)),
        name="out_proj",
    )(mix, w_out, x)


def _ffn_kernel(x_ref, g_ref, wu_ref, wd_ref, o_ref, xn_ref):
    f = pl.program_id(1)

    @pl.when(f == 0)
    def _():
        xn_ref[...] = _rms(x_ref[...], g_ref[...]).astype(BF16)
        o_ref[...] = x_ref[...]

    h = jnp.maximum(jnp.dot(xn_ref[...], wu_ref[...], preferred_element_type=F32), 0.0)
    o_ref[...] += jnp.dot((h * h).astype(BF16), wd_ref[...], preferred_element_type=F32)


def _ffn(x, g, w_up, w_down, l, tf=1024):
    return pl.pallas_call(
        _ffn_kernel,
        grid=(N_ROWS // TM, D_FF // tf),
        in_specs=[
            pl.BlockSpec((TM, D_MODEL), lambda m, f: (m, 0)),
            pl.BlockSpec((None, 1, D_MODEL), lambda m, f: (l, 0, 0)),
            pl.BlockSpec((None, D_MODEL, tf), lambda m, f: (l, 0, f)),
            pl.BlockSpec((None, tf, D_MODEL), lambda m, f: (l, f, 0)),
        ],
        out_specs=pl.BlockSpec((TM, D_MODEL), lambda m, f: (m, 0)),
        out_shape=jax.ShapeDtypeStruct((N_ROWS, D_MODEL), F32),
        scratch_shapes=[pltpu.VMEM((TM, D_MODEL), BF16)],
        compiler_params=_cparams(("parallel", "arbitrary")),
        name="ffn",
    )(x, g, w_up, w_down)


def _diff_lambda(lam_ref, lam_init):
    lf = lam_ref[...]
    s1 = jnp.sum(lf[0:1] * lf[1:2], axis=-1, keepdims=True)
    s2 = jnp.sum(lf[2:3] * lf[3:4], axis=-1, keepdims=True)
    return jnp.exp(s1) - jnp.exp(s2) + lam_init


def _diff_finish(o1, o2, lam, g, lam_init):
    return _rms(o1 - lam * o2, g) * (1.0 - lam_init)


def _block_diag_q(q, t):
    z = jnp.zeros((t, HEAD_DIM), q.dtype)
    top = jnp.concatenate([q[:, :HEAD_DIM], z], axis=1)
    bot = jnp.concatenate([z, q[:, HEAD_DIM:]], axis=1)
    return jnp.concatenate([top, bot], axis=0)


_NT = (((1,), (1,)), ((), ()))


def _prompt_attn_kernel(q_ref, k_ref, v_ref, km_ref, vm_ref, lam_ref, g_ref, o_ref,
                        kb_ref, vb_ref, s_ref, p_ref, m_ref, l_ref, a_ref, acc_ref, *, tq, lam_init):
    qi = pl.program_id(2)
    rows = 2 * tq

    @pl.when(qi == 0)
    def _():
        kb_ref[...] = k_ref[...].astype(BF16)
        vb_ref[...] = v_ref[...].astype(BF16)

    q = q_ref[...].astype(BF16)
    q1 = q[:, :HEAD_DIM]
    q2 = q[:, HEAD_DIM:]

    def step(kblk, vblk, nk, visible, first):
        s_ref[0:tq, 0:nk] = lax.dot_general(q1, kblk[:, :HEAD_DIM], _NT, preferred_element_type=F32)
        s_ref[tq:rows, 0:nk] = lax.dot_general(q2, kblk[:, HEAD_DIM:], _NT, preferred_element_type=F32)
        reps = nk // LANES
        for rc in range(rows // ROW_CHUNK):
            rs = slice(rc * ROW_CHUNK, (rc + 1) * ROW_CHUNK)
            s = s_ref[rs, 0:nk]
            if visible is not None:
                r = lax.broadcasted_iota(jnp.int32, (ROW_CHUNK, nk), 0) + (rc * ROW_CHUNK) % tq
                c = lax.broadcasted_iota(jnp.int32, (ROW_CHUNK, nk), 1)
                s = jnp.where(visible(r, c), s, NEG)
            m_new = jnp.broadcast_to(jnp.max(s, axis=-1, keepdims=True), (ROW_CHUNK, LANES))
            if not first:
                m_prev = m_ref[rs, :]
                m_new = jnp.maximum(m_prev, m_new)
                alpha = jnp.exp(m_prev - m_new)
                a_ref[rs, :] = alpha
            m_ref[rs, :] = m_new
            p = jnp.exp(s - jnp.concatenate([m_new] * reps, axis=1))
            p_ref[rs, 0:nk] = p.astype(BF16)
            psum = p[:, 0:LANES]
            for i in range(1, reps):
                psum = psum + p[:, i * LANES:(i + 1) * LANES]
            l_ref[rs, :] = psum if first else alpha * l_ref[rs, :] + psum
        pv = jnp.dot(p_ref[:, 0:nk], vblk, preferred_element_type=F32)
        if first:
            acc_ref[...] = pv
        else:
            alpha = a_ref[...]
            acc_ref[...] = jnp.concatenate([alpha, alpha], axis=1) * acc_ref[...] + pv

    km = km_ref[...].astype(BF16)
    vm = vm_ref[...].astype(BF16)

    @pl.when(qi % 2 == 0)
    def _():
        off = pl.multiple_of(qi * tq, tq)
        step(jnp.concatenate([km, kb_ref[pl.ds(off, tq), :]], axis=0),
             jnp.concatenate([vm, vb_ref[pl.ds(off, tq), :]], axis=0), LANES + tq,
             lambda r, c: jnp.logical_or(c < N_META, jnp.logical_and(c >= LANES, c - LANES <= r)), True)

    @pl.when(qi % 2 == 1)
    def _():
        off = pl.multiple_of((qi - 1) * tq, tq)
        step(jnp.concatenate([km, kb_ref[pl.ds(off, 2 * tq), :]], axis=0),
             jnp.concatenate([vm, vb_ref[pl.ds(off, 2 * tq), :]], axis=0), LANES + 2 * tq,
             lambda r, c: jnp.logical_or(c < N_META, jnp.logical_and(c >= LANES, c - LANES <= r + tq)), True)

    def body(j, carry):
        off = pl.multiple_of(j * (2 * tq), 2 * tq)
        step(kb_ref[pl.ds(off, 2 * tq), :], vb_ref[pl.ds(off, 2 * tq), :], 2 * tq, None, False)
        return carry

    lax.fori_loop(0, qi // 2, body, 0)

    inv_l = 1.0 / jnp.sum(l_ref[...], axis=-1, keepdims=True)
    o = acc_ref[...] * inv_l
    lam = _diff_lambda(lam_ref, lam_init)
    o_ref[...] = _diff_finish(o[:tq], o[tq:], lam, g_ref[...], lam_init).astype(o_ref.dtype)


def _prompt_attn(proj, kmeta, vmeta, lam, subln_g, l, lam_init, tq=512):
    nq = SEQ // tq
    kcol0 = DIFF_WIDTH // DIFF_VDIM
    return pl.pallas_call(
        functools.partial(_prompt_attn_kernel, tq=tq, lam_init=lam_init),
        grid=(BATCH, DIFF_HEADS, nq),
        in_specs=[
            pl.BlockSpec((tq, DIFF_VDIM), lambda b, h, i: (b * nq + i, h)),
            pl.BlockSpec((SEQ, DIFF_VDIM), lambda b, h, i: (b, kcol0 + h)),
            pl.BlockSpec((SEQ, DIFF_VDIM), lambda b, h, i: (b, 2 * kcol0 + h)),
            pl.BlockSpec((LANES, DIFF_VDIM), lambda b, h, i: (0, h)),
            pl.BlockSpec((LANES, DIFF_VDIM), lambda b, h, i: (0, h)),
            pl.BlockSpec((None, 4, HEAD_DIM), lambda b, h, i: (l, 0, 0)),
            pl.BlockSpec((None, 1, DIFF_VDIM), lambda b, h, i: (l, 0, 0)),
        ],
        out_specs=pl.BlockSpec((tq, DIFF_VDIM), lambda b, h, i: (b * nq + i, h)),
        out_shape=jax.ShapeDtypeStruct((ROWS_PROMPT, DIFF_WIDTH), BF16),
        scratch_shapes=[
            pltpu.VMEM((SEQ, DIFF_VDIM), BF16),
            pltpu.VMEM((SEQ, DIFF_VDIM), BF16),
            pltpu.VMEM((2 * tq, 2 * tq + LANES), F32),
            pltpu.VMEM((2 * tq, 2 * tq + LANES), BF16),
            pltpu.VMEM((2 * tq, LANES), F32),
            pltpu.VMEM((2 * tq, LANES), F32),
            pltpu.VMEM((2 * tq, LANES), F32),
            pltpu.VMEM((2 * tq, DIFF_VDIM), F32),
        ],
        compiler_params=_cparams(("parallel", "parallel", "arbitrary")),
        name="prompt_attn",
    )(proj, proj, proj, kmeta, vmeta, lam, subln_g)


def _meta_attn_kernel(q_ref, km_ref, vm_ref, lam_ref, g_ref, o_ref, *, lam_init):
    t = N_META
    lam = _diff_lambda(lam_ref, lam_init)
    row = lax.broadcasted_iota(jnp.int32, (2 * t, LANES), 0)
    row = jnp.where(row >= t, row - t, row)
    col = lax.broadcasted_iota(jnp.int32, (2 * t, LANES), 1)
    for h in range(DIFF_HEADS):
        sl = slice(h * DIFF_VDIM, (h + 1) * DIFF_VDIM)
        qbd = _block_diag_q(q_ref[:, sl], t).astype(BF16)
        s = lax.dot_general(qbd, km_ref[:, sl].astype(BF16), _NT, preferred_element_type=F32)
        s = jnp.where(col <= row, s, NEG)
        p = jnp.exp(s - jnp.max(s, axis=-1, keepdims=True))
        o = jnp.dot(p.astype(BF16), vm_ref[:, sl].astype(BF16), preferred_element_type=F32)
        o = o / jnp.sum(p, axis=-1, keepdims=True)
        o_ref[:, sl] = _diff_finish(o[:t], o[t:], lam, g_ref[...], lam_init)


def _meta_attn(qmeta, kmeta, vmeta, lam, subln_g, l, lam_init):
    return pl.pallas_call(
        functools.partial(_meta_attn_kernel, lam_init=lam_init),
        grid=(1,),
        in_specs=[
            pl.BlockSpec((N_META, DIFF_WIDTH), lambda i: (0, 0)),
            pl.BlockSpec((LANES, DIFF_WIDTH), lambda i: (0, 0)),
            pl.BlockSpec((LANES, DIFF_WIDTH), lambda i: (0, 0)),
            pl.BlockSpec((None, 4, HEAD_DIM), lambda i: (l, 0, 0)),
            pl.BlockSpec((None, 1, DIFF_VDIM), lambda i: (l, 0, 0)),
        ],
        out_specs=pl.BlockSpec((N_META, DIFF_WIDTH), lambda i: (0, 0)),
        out_shape=jax.ShapeDtypeStruct((N_META, DIFF_WIDTH), F32),
        compiler_params=_cparams(("arbitrary",)),
        name="meta_attn",
    )(qmeta, kmeta, vmeta, lam, subln_g)


def _by_subhead(x):
    return jnp.concatenate([x[:, h * DIFF_VDIM + c * HEAD_DIM:h * DIFF_VDIM + (c + 1) * HEAD_DIM]
                            for c in range(2) for h in range(DIFF_HEADS)], axis=0)


def _sample_attn_kernel(pt_ref, q_ref, kn_ref, vn_ref, *rest, pp, lam_init):
    kp_refs = rest[:pp]
    vp_refs = rest[pp:2 * pp]
    lam_ref, g_ref, o_ref, qs_ref, m_ref, l_ref, acc_ref = rest[2 * pp:]
    t = DEC_SEQ
    nq = NSUB * t
    g = pl.program_id(1)

    @pl.when(g == 0)
    def _():
        qs_ref[...] = _by_subhead(q_ref[...]).astype(BF16)
        m_ref[...] = jnp.full_like(m_ref, NEG)
        l_ref[...] = jnp.zeros_like(l_ref)
        acc_ref[...] = jnp.zeros_like(acc_ref)

    def attend(k_blocks, v_blocks, bias, dist):
        s = []
        for kb in k_blocks:
            sk = lax.dot_general(qs_ref[...], kb, _NT, preferred_element_type=F32)
            s.append([sk[:, i * LANES:(i + 1) * LANES] + bias for i in range(sk.shape[1] // LANES)])
        m_prev = m_ref[...]
        m_tile = None
        for sk in s:
            for st in sk:
                m_tile = st if m_tile is None else jnp.maximum(m_tile, st)
        m_new = jnp.maximum(m_prev, jnp.max(m_tile, axis=-1, keepdims=True))
        alpha = jnp.exp(m_prev - m_new)
        lane = lax.broadcasted_iota(jnp.int32, (nq, LANES), 1)
        low = (lane & dist) == 0
        l_tile = None
        pv = None
        for sk, vb in zip(s, v_blocks):
            p2 = []
            for st in sk:
                p = jnp.exp(st - m_new)
                l_tile = p if l_tile is None else l_tile + p
                sw = jnp.where(low, pltpu.roll(p, LANES - dist, 1), pltpu.roll(p, dist, 1))
                p2.append(jnp.concatenate([p, sw], axis=0).astype(BF16))
            d = jnp.dot(jnp.concatenate(p2, axis=1), vb, preferred_element_type=F32)
            pv = d if pv is None else pv + d
        acc_ref[...] = jnp.concatenate([alpha, alpha], axis=0) * acc_ref[...] + pv
        l_ref[...] = alpha * l_ref[...] + jnp.sum(l_tile, axis=-1, keepdims=True)
        m_ref[...] = m_new

    row = lax.broadcasted_iota(jnp.int32, (nq, LANES), 0)
    col = lax.broadcasted_iota(jnp.int32, (nq, LANES), 1)
    bias = jnp.where((col & (NSUB - 1)) == row // t, 0.0, NEG)
    attend([r[...].astype(BF16) for r in kp_refs], [r[...].astype(BF16) for r in vp_refs], bias, DIFF_HEADS)

    @pl.when(g == pl.num_programs(1) - 1)
    def _():
        zpad = jnp.zeros((LANES - nq, HEAD_DIM), F32)
        kn = jnp.concatenate([_by_subhead(kn_ref[...]), zpad], axis=0).astype(BF16)
        vn = jnp.concatenate([_by_subhead(vn_ref[...]), zpad], axis=0).astype(BF16)
        r = lax.broadcasted_iota(jnp.int32, (nq, LANES), 0)
        c = lax.broadcasted_iota(jnp.int32, (nq, LANES), 1)
        valid = jnp.logical_and(c // t == r // t, c % t <= r % t)
        attend([kn], [vn], jnp.where(valid, 0.0, NEG), DIFF_HEADS * t)
        lam = _diff_lambda(lam_ref, lam_init)
        o = acc_ref[...] / jnp.concatenate([l_ref[...], l_ref[...]], axis=0)
        for h in range(DIFF_HEADS):
            r1 = h * t
            r2 = (DIFF_HEADS + h) * t
            o1 = jnp.concatenate([o[r1:r1 + t], o[nq + r1:nq + r1 + t]], axis=1)
            o2 = jnp.concatenate([o[nq + r2:nq + r2 + t], o[r2:r2 + t]], axis=1)
            o_ref[:, h * DIFF_VDIM:(h + 1) * DIFF_VDIM] = _diff_finish(o1, o2, lam, g_ref[...], lam_init)


def _sample_attn(page_table, proj, cache_k, cache_v, lam, subln_g, l, lam_init, pp=16):
    n_pages = page_table.shape[1]
    row0 = ROW_SAMPLE0 // DEC_SEQ
    nq = NSUB * DEC_SEQ
    qkv_spec = lambda c: pl.BlockSpec((DEC_SEQ, DIFF_WIDTH), lambda b, g, pt: (row0 + b, c))
    page_spec = lambda i: pl.BlockSpec((None, None, PAGE_ROWS, HEAD_DIM),
                                       lambda b, g, pt: (l, pt[b, g * pp + i], 0, 0))
    grid_spec = pltpu.PrefetchScalarGridSpec(
        num_scalar_prefetch=1,
        grid=(DEC_BATCH, n_pages // pp),
        in_specs=[qkv_spec(0), qkv_spec(1), qkv_spec(2)]
        + [page_spec(i) for i in range(pp)] + [page_spec(i) for i in range(pp)]
        + [pl.BlockSpec((None, 4, HEAD_DIM), lambda b, g, pt: (l, 0, 0)),
           pl.BlockSpec((None, 1, DIFF_VDIM), lambda b, g, pt: (l, 0, 0))],
        out_specs=pl.BlockSpec((DEC_SEQ, DIFF_WIDTH), lambda b, g, pt: (b, 0)),
        scratch_shapes=[
            pltpu.VMEM((nq, HEAD_DIM), BF16),
            pltpu.VMEM((nq, 1), F32),
            pltpu.VMEM((nq, 1), F32),
            pltpu.VMEM((2 * nq, HEAD_DIM), F32),
        ],
    )
    return pl.pallas_call(
        functools.partial(_sample_attn_kernel, pp=pp, lam_init=lam_init),
        grid_spec=grid_spec,
        out_shape=jax.ShapeDtypeStruct((ROWS_SAMPLE, DIFF_WIDTH), F32),
        compiler_params=_cparams(("parallel", "arbitrary")),
        name="sample_attn",
    )(page_table, proj, proj, proj, *([cache_k] * pp), *([cache_v] * pp), lam, subln_g)


def _page_view(cache):
    d, n = cache.shape[:2]
    c = cache.reshape(d, n, PAGE_SIZE, DIFF_HEADS, 2, HEAD_DIM)
    return jnp.transpose(c, (0, 1, 2, 4, 3, 5)).reshape(d, n, PAGE_ROWS, HEAD_DIM)


def _dn_kernel(xq_ref, xk_ref, xv_ref, z_ref, ba_ref, wq_ref, wk_ref, wv_ref, alog_ref, dt_ref, ng_ref,
               sinit_ref, cinit_ref, o_ref, sout_ref, cout_ref, xp_ref, s_ref, *, nv, hg):
    c = pl.program_id(2)
    last = c == pl.num_programs(2) - 1
    width = hg * DN_DK
    x_refs = (xq_ref, xk_ref, xv_ref)
    w_refs = (wq_ref, wk_ref, wv_ref)
    heads = range(hg)

    @pl.when(c == 0)
    def _():
        s_ref[...] = sinit_ref[...]
        for p in range(3):
            xp_ref[p, 0:8, :] = cinit_ref[:, p * width:(p + 1) * width]

    ys = []
    for p in range(3):
        xp_ref[p, 8:8 + nv, :] = x_refs[p][...]
        if nv < CHUNK:
            xp_ref[p, 8 + nv:, :] = jnp.zeros((CHUNK - nv, width), F32)
        xpv = xp_ref[p]
        y = xpv[8:, :] * w_refs[p][CONV_K - 1:CONV_K, :]
        for d in range(1, CONV_K):
            y = y + pltpu.roll(xpv, d, 0)[8:, :] * w_refs[p][CONV_K - 1 - d:CONV_K - d, :]
        ys.append(y * jax.nn.sigmoid(y))

    @pl.when(last)
    def _():
        for p in range(3):
            cout_ref[:, p * width:(p + 1) * width] = xp_ref[p, nv:nv + 8, :]

    if nv == CHUNK:
        for p in range(3):
            xp_ref[p, 0:8, :] = xp_ref[p, CHUNK:CHUNK + 8, :]

    ba = ba_ref[...]
    if nv < CHUNK:
        ba = jnp.concatenate([ba, jnp.zeros((CHUNK - nv, LANES), F32)], axis=0)
    gates = ba.T[0:2 * DN_HEADS, :]
    valid = lax.broadcasted_iota(jnp.int32, gates.shape, 1) < nv
    is_beta = lax.broadcasted_iota(jnp.int32, gates.shape, 0) < DN_HEADS
    ab = gates + dt_ref[...]
    softplus = jnp.maximum(ab, 0.0) + jnp.log1p(jnp.exp(-jnp.abs(ab)))
    g_t = jnp.where(jnp.logical_and(valid, jnp.logical_not(is_beta)), -jnp.exp(alog_ref[...]) * softplus, 0.0)
    ri = lax.broadcasted_iota(jnp.int32, (CHUNK, CHUNK), 0)
    ci = lax.broadcasted_iota(jnp.int32, (CHUNK, CHUNK), 1)
    causal = ri >= ci
    eye = (ri == ci).astype(F32)
    cum_t = jnp.dot(g_t, (ri <= ci).astype(F32), preferred_element_type=F32, precision=lax.Precision.HIGHEST)
    slab = jnp.where(is_beta, jnp.where(valid, jax.nn.sigmoid(gates), 0.0), cum_t)
    cols = jnp.concatenate([slab, jnp.zeros((CHUNK - 2 * DN_HEADS, CHUNK), F32)], axis=0).T

    z = z_ref[...]
    if nv < CHUNK:
        z = jnp.concatenate([z, jnp.zeros((CHUNK - nv, width), F32)], axis=0)

    sl = [slice(j * DN_DK, (j + 1) * DN_DK) for j in heads]
    beta = [cols[:, j:j + 1] for j in heads]
    gc = [cols[:, DN_HEADS + j:DN_HEADS + j + 1] for j in heads]
    gr = [cum_t[DN_HEADS + j:DN_HEADS + j + 1, :] for j in heads]
    g_last = [gc[j][CHUNK - 1:CHUNK, :] for j in heads]

    def l2n(x, scale):
        return x * (lax.rsqrt(jnp.sum(x * x, axis=-1, keepdims=True) + EPS) * scale)

    q = [l2n(ys[0][:, sl[j]], DN_DK ** -0.5) for j in heads]
    k = [l2n(ys[1][:, sl[j]], 1.0) for j in heads]
    v = [ys[2][:, sl[j]] for j in heads]
    k_t = [k[j].T for j in heads]
    s_old = [s_ref[j] for j in heads]
    qk16 = [jnp.concatenate([q[j], k[j]], axis=0).astype(BF16) for j in heads]
    a1 = [jnp.dot(qk16[j], k_t[j].astype(BF16), preferred_element_type=F32) for j in heads]
    a2 = [jnp.dot(qk16[j], s_old[j].astype(BF16), preferred_element_type=F32) for j in heads]
    decay = [jnp.exp(jnp.where(causal, gc[j] - gr[j], NEG)) for j in heads]
    e_g = [jnp.exp(gc[j]) for j in heads]
    rhs = [(beta[j] * (v[j] - e_g[j] * a2[j][CHUNK:])).astype(BF16) for j in heads]
    xpow = [a1[j][CHUNK:] * ((decay[j] - eye) * (-beta[j])) for j in heads]
    t_inv = [eye + xpow[j] for j in heads]
    xb = [xpow[j].astype(BF16) for j in heads]
    for _ in range(int(math.log2(CHUNK)) - 1):
        xb = [jnp.dot(xb[j], xb[j], preferred_element_type=F32).astype(BF16) for j in heads]
        t_inv = [t_inv[j] + jnp.dot(t_inv[j].astype(BF16), xb[j], preferred_element_type=F32) for j in heads]
    u = [jnp.dot(t_inv[j].astype(BF16), rhs[j], preferred_element_type=F32).astype(BF16) for j in heads]
    o = [e_g[j] * a2[j][:CHUNK] + jnp.dot((a1[j][:CHUNK] * decay[j]).astype(BF16), u[j],
                                          preferred_element_type=F32) for j in heads]
    for j in heads:
        tail = jnp.exp(g_last[j] - gr[j])
        s_ref[j] = jnp.exp(g_last[j]) * s_old[j] + jnp.dot((k_t[j] * tail).astype(BF16), u[j],
                                                           preferred_element_type=F32)
    for j in heads:
        zj = z[:, sl[j]]
        on = _rms(o[j], ng_ref[...]) * (zj * jax.nn.sigmoid(zj))
        o_ref[:, sl[j]] = on[:nv].astype(o_ref.dtype)

    @pl.when(last)
    def _():
        sout_ref[...] = s_ref[...]


def _dn(proj, tail, conv_w, alog_rows, dt_rows, ng, s_init, c_init, l, *, nv, row0, nseq, nchunk, shared_init,
        out_dtype):
    hg = DN_HEADS
    width = hg * DN_DK
    q0 = 3 * DIFF_WIDTH // width
    rb0 = row0 // nv
    row_map = lambda s, g, c: rb0 + s * nchunk + c
    init_map = (lambda s: 0) if shared_init else (lambda s: s)
    x_spec = lambda sec: pl.BlockSpec((nv, width), lambda s, g, c: (row_map(s, g, c), q0 + sec))
    w_spec = lambda sec: pl.BlockSpec((None, CONV_K, width), lambda s, g, c: (l, 0, sec))
    return pl.pallas_call(
        functools.partial(_dn_kernel, nv=nv, hg=hg),
        grid=(nseq, 1, nchunk),
        in_specs=[
            x_spec(0), x_spec(1), x_spec(2), x_spec(3),
            pl.BlockSpec((nv, LANES), lambda s, g, c: (row_map(s, g, c), 0)),
            w_spec(0), w_spec(1), w_spec(2),
            pl.BlockSpec((None, 2 * DN_HEADS, LANES), lambda s, g, c: (l, 0, 0)),
            pl.BlockSpec((None, 2 * DN_HEADS, LANES), lambda s, g, c: (l, 0, 0)),
            pl.BlockSpec((None, 1, DN_DK), lambda s, g, c: (l, 0, 0)),
            pl.BlockSpec((None, hg, DN_DK, DN_DK), lambda s, g, c: (init_map(s), 0, 0, 0)),
            pl.BlockSpec((None, 8, 3 * DN_WIDTH), lambda s, g, c: (init_map(s), 0, 0)),
        ],
        out_specs=[
            pl.BlockSpec((nv, width), lambda s, g, c: (s * nchunk + c, 0)),
            pl.BlockSpec((None, hg, DN_DK, DN_DK), lambda s, g, c: (s, 0, 0, 0)),
            pl.BlockSpec((None, 8, 3 * DN_WIDTH), lambda s, g, c: (s, 0, 0)),
        ],
        out_shape=[
            jax.ShapeDtypeStruct((nseq * nchunk * nv, DN_WIDTH), out_dtype),
            jax.ShapeDtypeStruct((nseq, DN_HEADS, DN_DK, DN_DK), F32),
            jax.ShapeDtypeStruct((nseq, 8, 3 * DN_WIDTH), F32),
        ],
        scratch_shapes=[
            pltpu.VMEM((3, CHUNK + 8, width), F32),
            pltpu.VMEM((hg, DN_DK, DN_DK), F32),
        ],
        compiler_params=_cparams(("parallel", "arbitrary", "arbitrary")),
        name=f"deltanet_{nv}",
    )(proj, proj, proj, proj, tail, conv_w, conv_w, conv_w, alog_rows, dt_rows, ng, s_init, c_init)


def kernel(x_prompt, x_sample, cache_k, cache_v, state_ssm, state_conv, page_table, meta_tokens, norm1_g, w_in,
           q_norm_g, k_norm_g, diff_lambda, subln_g, conv_w, a_log, dt_bias, dn_norm_g, w_out, norm2_g, w_up,
           w_down):
    cache_k = _page_view(cache_k)
    cache_v = _page_view(cache_v)

    w_main = w_in.astype(BF16)
    w_tail = jnp.pad(w_in[:, :, MAIN_COLS:], ((0, 0), (0, 0), (0, LANES - 2 * DN_HEADS))).astype(BF16)
    w_out_b = w_out.astype(BF16)
    w_up_b = w_up.astype(BF16)
    w_down_b = w_down.astype(BF16)
    vec = lambda a: a.reshape(DEPTH, 1, a.shape[-1])
    gate_rows = lambda a: jnp.broadcast_to(jnp.pad(a, ((0, 0), (DN_HEADS, 0)))[:, :, None],
                                           (DEPTH, 2 * DN_HEADS, LANES))
    alog_rows = gate_rows(a_log)
    dt_rows = gate_rows(dt_bias)
    norm1_g, norm2_g, q_norm_g, k_norm_g = vec(norm1_g), vec(norm2_g), vec(q_norm_g), vec(k_norm_g)
    subln_g, dn_norm_g = vec(subln_g), vec(dn_norm_g)
    conv_state = jnp.pad(state_conv, ((0, 0), (0, 0), (8 - (CONV_K - 1), 0), (0, 0)))

    x = jnp.concatenate([
        x_prompt.reshape(ROWS_PROMPT, D_MODEL),
        x_sample.reshape(ROWS_SAMPLE, D_MODEL),
        meta_tokens,
        jnp.zeros((N_ROWS - ROW_META0 - N_META, D_MODEL), F32),
    ], axis=0)

    zero_state = jnp.zeros((1, DN_HEADS, DN_DK, DN_DK), F32)
    zero_conv = jnp.zeros((1, 8, 3 * DN_WIDTH), F32)
    outs = [[] for _ in range(8)]
    for l in range(DEPTH):
        lam_init = 0.8 - 0.6 * math.exp(-0.3 * l)
        proj, tail, k2, v2 = _in_proj(x, norm1_g, w_main, w_tail, q_norm_g, k_norm_g, l)
        k_all = proj[:, DIFF_WIDTH:2 * DIFF_WIDTH]
        v_all = proj[:, 2 * DIFF_WIDTH:3 * DIFF_WIDTH]
        meta = slice(ROW_META0, ROW_META0 + N_META)
        kmeta = jnp.pad(k_all[meta], ((0, LANES - N_META), (0, 0)))
        vmeta = jnp.pad(v_all[meta], ((0, LANES - N_META), (0, 0)))

        att_p = _prompt_attn(proj, kmeta, vmeta, diff_lambda, subln_g, l, lam_init)
        att_m = _meta_attn(proj[meta, :DIFF_WIDTH], kmeta, vmeta, diff_lambda, subln_g, l, lam_init)
        att_s = _sample_attn(page_table, proj, cache_k, cache_v, diff_lambda, subln_g, l, lam_init)

        dn_args = (proj, tail, conv_w, alog_rows, dt_rows, dn_norm_g)
        dn_m, s_meta, c_meta = _dn(*dn_args, zero_state, zero_conv, l, nv=N_META, row0=ROW_META0, nseq=1, nchunk=1,
                                   shared_init=True, out_dtype=F32)
        dn_s, s_samp, c_samp = _dn(*dn_args, state_ssm[l], conv_state[l], l, nv=DEC_SEQ, row0=ROW_SAMPLE0,
                                   nseq=DEC_BATCH, nchunk=1, shared_init=False, out_dtype=F32)
        dn_p, s_prom, c_prom = _dn(*dn_args, s_meta, c_meta, l, nv=CHUNK, row0=0, nseq=BATCH, nchunk=SEQ // CHUNK,
                                   shared_init=True, out_dtype=BF16)

        mix = jnp.concatenate([
            jnp.concatenate([att_p, dn_p], axis=1),
            jnp.concatenate([att_s, dn_s], axis=1).astype(BF16),
            jnp.concatenate([att_m, dn_m], axis=1).astype(BF16),
            jnp.zeros((N_ROWS - ROW_META0 - N_META, DIFF_WIDTH + DN_WIDTH), BF16),
        ], axis=0)
        x = _out_proj(mix, w_out_b, x, l)
        x = _ffn(x, norm2_g, w_up_b, w_down_b, l)

        def unpaged(t2, rows):
            t = t2[rows.start * NSUB:rows.stop * NSUB].reshape(rows.stop - rows.start, 2, DIFF_HEADS, HEAD_DIM)
            return jnp.transpose(t, (0, 2, 1, 3)).reshape(rows.stop - rows.start, DIFF_HEADS, DIFF_VDIM)

        def with_meta(t2):
            real = unpaged(t2, slice(0, ROWS_PROMPT)).reshape(BATCH, SEQ, DIFF_HEADS, DIFF_VDIM)
            m = jnp.broadcast_to(unpaged(t2, meta)[None], (BATCH, N_META, DIFF_HEADS, DIFF_VDIM))
            return jnp.concatenate([m, real], axis=1)

        samp = slice(ROW_SAMPLE0, ROW_SAMPLE0 + ROWS_SAMPLE)
        outs[0].append(with_meta(k2))
        outs[1].append(with_meta(v2))
        outs[2].append(unpaged(k2, samp).reshape(DEC_BATCH, DEC_SEQ, DIFF_HEADS, DIFF_VDIM))
        outs[3].append(unpaged(v2, samp).reshape(DEC_BATCH, DEC_SEQ, DIFF_HEADS, DIFF_VDIM))
        outs[4].append(s_prom)
        outs[5].append(s_samp)
        outs[6].append(c_prom[:, 8 - (CONV_K - 1):])
        outs[7].append(c_samp[:, 8 - (CONV_K - 1):])

    y_prompt = x[:ROWS_PROMPT].reshape(BATCH, SEQ, D_MODEL)
    y_sample = x[ROW_SAMPLE0:ROW_SAMPLE0 + ROWS_SAMPLE].reshape(DEC_BATCH, DEC_SEQ, D_MODEL)
    return (y_prompt, y_sample) + tuple(jnp.stack(o) for o in outs)
```

```python
import functools
import math

import jax
import jax.numpy as jnp
from jax import lax
from jax.experimental import pallas as pl
from jax.experimental.pallas import tpu as pltpu

F32 = jnp.float32
BF16 = jnp.bfloat16

D_MODEL = 2048
BATCH = 4
SEQ = 2048
DEPTH = 4
DEC_BATCH = 8
DEC_SEQ = 8
PAGE_SIZE = 128
N_META = 16
HEAD_DIM = 128
DIFF_HEADS = 4
DIFF_VDIM = 2 * HEAD_DIM
DIFF_WIDTH = DIFF_HEADS * DIFF_VDIM
DN_HEADS = 8
DN_DK = 128
DN_WIDTH = DN_HEADS * DN_DK
MIX_WIDTH = DIFF_WIDTH + DN_WIDTH
CONV_K = 4
D_FF = 4 * D_MODEL
EPS = 1e-6
MAIN_COLS = 3 * DIFF_WIDTH + 4 * DN_WIDTH

ROWS_PROMPT = BATCH * SEQ
ROWS_SAMPLE = DEC_BATCH * DEC_SEQ
ROW_SAMPLE0 = ROWS_PROMPT
ROW_META0 = ROWS_PROMPT + ROWS_SAMPLE
N_ROWS = 8320
TM = 640
TM_IN = 1040
LANES = 128
CHUNK = 128
NEG = -1e30
PAGE_ROWS = PAGE_SIZE * 2 * DIFF_HEADS
NSUB = 2 * DIFF_HEADS
ROW_CHUNK = 32
VMEM_LIMIT = 56 * 1024 * 1024


def _cparams(sem):
    return pltpu.CompilerParams(dimension_semantics=sem, vmem_limit_bytes=VMEM_LIMIT)


def _rms(x, gain):
    return x * lax.rsqrt(jnp.mean(x * x, axis=-1, keepdims=True) + EPS) * gain


def _store_paged(dst_ref, cols, tile):
    n_rows = cols.shape[0]
    for j in range(cols.shape[1] // HEAD_DIM):
        piece = tile * (cols.shape[1] // HEAD_DIM) + j
        dst_ref[pl.ds((piece % 2) * DIFF_HEADS + piece // 2, n_rows, stride=NSUB), :] = (
            cols[:, j * HEAD_DIM:(j + 1) * HEAD_DIM])


def _in_proj_kernel(x_ref, g_ref, w_ref, wt_ref, qg_ref, kg_ref, o_ref, ot_ref, k2_ref, v2_ref, xn_ref, *,
                    n_q, n_k):
    n = pl.program_id(1)

    @pl.when(n == 0)
    def _():
        xn_ref[...] = _rms(x_ref[...], g_ref[...]).astype(BF16)
        ot_ref[...] = jnp.dot(xn_ref[...], wt_ref[...], preferred_element_type=F32)

    acc = jnp.dot(xn_ref[...], w_ref[...], preferred_element_type=F32)

    def head_norm(gain):
        cols = [_rms(acc[:, j * HEAD_DIM:(j + 1) * HEAD_DIM], gain) for j in range(acc.shape[1] // HEAD_DIM)]
        return jnp.concatenate(cols, axis=1)

    @pl.when(n < n_q)
    def _():
        o_ref[...] = head_norm(qg_ref[...] * (HEAD_DIM ** -0.5))

    for t in range(n_k):
        @pl.when(n == n_q + t)
        def _():
            kn = head_norm(kg_ref[...])
            o_ref[...] = kn
            _store_paged(k2_ref, kn, t)

        @pl.when(n == n_q + n_k + t)
        def _():
            o_ref[...] = acc
            _store_paged(v2_ref, acc, t)

    @pl.when(n >= n_q + 2 * n_k)
    def _():
        o_ref[...] = acc


def _in_proj(x, g, w_main, w_tail, qg, kg, l, tn=512):
    n_q = DIFF_WIDTH // tn
    return pl.pallas_call(
        functools.partial(_in_proj_kernel, n_q=n_q, n_k=n_q),
        grid=(N_ROWS // TM_IN, MAIN_COLS // tn),
        in_specs=[
            pl.BlockSpec((TM_IN, D_MODEL), lambda m, n: (m, 0)),
            pl.BlockSpec((None, 1, D_MODEL), lambda m, n: (l, 0, 0)),
            pl.BlockSpec((None, D_MODEL, tn), lambda m, n: (l, 0, n)),
            pl.BlockSpec((None, D_MODEL, LANES), lambda m, n: (l, 0, 0)),
            pl.BlockSpec((None, 1, HEAD_DIM), lambda m, n: (l, 0, 0)),
            pl.BlockSpec((None, 1, HEAD_DIM), lambda m, n: (l, 0, 0)),
        ],
        out_specs=[
            pl.BlockSpec((TM_IN, tn), lambda m, n: (m, n)),
            pl.BlockSpec((TM_IN, LANES), lambda m, n: (m, 0)),
            pl.BlockSpec((TM_IN * NSUB, HEAD_DIM), lambda m, n: (m, 0)),
            pl.BlockSpec((TM_IN * NSUB, HEAD_DIM), lambda m, n: (m, 0)),
        ],
        out_shape=[
            jax.ShapeDtypeStruct((N_ROWS, MAIN_COLS), F32),
            jax.ShapeDtypeStruct((N_ROWS, LANES), F32),
            jax.ShapeDtypeStruct((N_ROWS * NSUB, HEAD_DIM), F32),
            jax.ShapeDtypeStruct((N_ROWS * NSUB, HEAD_DIM), F32),
        ],
        scratch_shapes=[pltpu.VMEM((TM_IN, D_MODEL), BF16)],
        compiler_params=_cparams(("parallel", "arbitrary")),
        name="in_proj",
    )(x, g, w_main, w_tail, qg, kg)


def _out_proj_kernel(mix_ref, w_ref, x_ref, o_ref):
    o_ref[...] = x_ref[...] + jnp.dot(mix_ref[...], w_ref[...], preferred_element_type=F32)


def _out_proj(mix, w_out, x, l):
    k = mix.shape[1]
    return pl.pallas_call(
        _out_proj_kernel,
        grid=(N_ROWS // TM,),
        in_specs=[
            pl.BlockSpec((TM, k), lambda m: (m, 0)),
            pl.BlockSpec((None, k, D_MODEL), lambda m: (l, 0, 0)),
            pl.BlockSpec((TM, D_MODEL), lambda m: (m, 0)),
        ],
        out_specs=pl.BlockSpec((TM, D_MODEL), lambda m: (m, 0)),
        out_shape=jax.ShapeDtypeStruct((N_ROWS, D_MODEL), F32),
        compiler_params=_cparams(("parallel",)),
        name="out_proj",
    )(mix, w_out, x)


def _ffn_kernel(x_ref, g_ref, wu_ref, wd_ref, o_ref, xn_ref):
    f = pl.program_id(1)

    @pl.when(f == 0)
    def _():
        xn_ref[...] = _rms(x_ref[...], g_ref[...]).astype(BF16)
        o_ref[...] = x_ref[...]

    h = jnp.maximum(jnp.dot(xn_ref[...], wu_ref[...], preferred_element_type=F32), 0.0)
    o_ref[...] += jnp.dot((h * h).astype(BF16), wd_ref[...], preferred_element_type=F32)


def _ffn(x, g, w_up, w_down, l, tf=1024):
    return pl.pallas_call(
        _ffn_kernel,
        grid=(N_ROWS // TM, D_FF // tf),
        in_specs=[
            pl.BlockSpec((TM, D_MODEL), lambda m, f: (m, 0)),
            pl.BlockSpec((None, 1, D_MODEL), lambda m, f: (l, 0, 0)),
            pl.BlockSpec((None, D_MODEL, tf), lambda m, f: (l, 0, f)),
            pl.BlockSpec((None, tf, D_MODEL), lambda m, f: (l, f, 0)),
        ],
        out_specs=pl.BlockSpec((TM, D_MODEL), lambda m, f: (m, 0)),
        out_shape=jax.ShapeDtypeStruct((N_ROWS, D_MODEL), F32),
        scratch_shapes=[pltpu.VMEM((TM, D_MODEL), BF16)],
        compiler_params=_cparams(("parallel", "arbitrary")),
        name="ffn",
    )(x, g, w_up, w_down)


def _diff_lambda(lam_ref, lam_init):
    lf = lam_ref[...]
    s1 = jnp.sum(lf[0:1] * lf[1:2], axis=-1, keepdims=True)
    s2 = jnp.sum(lf[2:3] * lf[3:4], axis=-1, keepdims=True)
    return jnp.exp(s1) - jnp.exp(s2) + lam_init


def _diff_finish(o1, o2, lam, g, lam_init):
    return _rms(o1 - lam * o2, g) * (1.0 - lam_init)


def _block_diag_q(q, t):
    z = jnp.zeros((t, HEAD_DIM), q.dtype)
    top = jnp.concatenate([q[:, :HEAD_DIM], z], axis=1)
    bot = jnp.concatenate([z, q[:, HEAD_DIM:]], axis=1)
    return jnp.concatenate([top, bot], axis=0)


_NT = (((1,), (1,)), ((), ()))


def _prompt_attn_kernel(q_ref, k_ref, v_ref, km_ref, vm_ref, lam_ref, g_ref, o_ref,
                        kb_ref, vb_ref, s_ref, p_ref, m_ref, l_ref, a_ref, acc_ref, *, tq, lam_init):
    qi = pl.program_id(2)
    rows = 2 * tq

    @pl.when(qi == 0)
    def _():
        kb_ref[...] = k_ref[...].astype(BF16)
        vb_ref[...] = v_ref[...].astype(BF16)

    q = q_ref[...].astype(BF16)
    q1 = q[:, :HEAD_DIM]
    q2 = q[:, HEAD_DIM:]

    def step(kblk, vblk, nk, visible, first):
        s_ref[0:tq, 0:nk] = lax.dot_general(q1, kblk[:, :HEAD_DIM], _NT, preferred_element_type=F32)
        s_ref[tq:rows, 0:nk] = lax.dot_general(q2, kblk[:, HEAD_DIM:], _NT, preferred_element_type=F32)
        reps = nk // LANES
        for rc in range(rows // ROW_CHUNK):
            rs = slice(rc * ROW_CHUNK, (rc + 1) * ROW_CHUNK)
            s = s_ref[rs, 0:nk]
            if visible is not None:
                r = lax.broadcasted_iota(jnp.int32, (ROW_CHUNK, nk), 0) + (rc * ROW_CHUNK) % tq
                c = lax.broadcasted_iota(jnp.int32, (ROW_CHUNK, nk), 1)
                s = jnp.where(visible(r, c), s, NEG)
            m_new = jnp.broadcast_to(jnp.max(s, axis=-1, keepdims=True), (ROW_CHUNK, LANES))
            if not first:
                m_prev = m_ref[rs, :]
                m_new = jnp.maximum(m_prev, m_new)
                alpha = jnp.exp(m_prev - m_new)
                a_ref[rs, :] = alpha
            m_ref[rs, :] = m_new
            p = jnp.exp(s - jnp.concatenate([m_new] * reps, axis=1))
            p_ref[rs, 0:nk] = p.astype(BF16)
            psum = p[:, 0:LANES]
            for i in range(1, reps):
                psum = psum + p[:, i * LANES:(i + 1) * LANES]
            l_ref[rs, :] = psum if first else alpha * l_ref[rs, :] + psum
        pv = jnp.dot(p_ref[:, 0:nk], vblk, preferred_element_type=F32)
        if first:
            acc_ref[...] = pv
        else:
            alpha = a_ref[...]
            acc_ref[...] = jnp.concatenate([alpha, alpha], axis=1) * acc_ref[...] + pv

    km = km_ref[...].astype(BF16)
    vm = vm_ref[...].astype(BF16)

    @pl.when(qi % 2 == 0)
    def _():
        off = pl.multiple_of(qi * tq, tq)
        step(jnp.concatenate([km, kb_ref[pl.ds(off, tq), :]], axis=0),
             jnp.concatenate([vm, vb_ref[pl.ds(off, tq), :]], axis=0), LANES + tq,
             lambda r, c: jnp.logical_or(c < N_META, jnp.logical_and(c >= LANES, c - LANES <= r)), True)

    @pl.when(qi % 2 == 1)
    def _():
        off = pl.multiple_of((qi - 1) * tq, tq)
        step(jnp.concatenate([km, kb_ref[pl.ds(off, 2 * tq), :]], axis=0),
             jnp.concatenate([vm, vb_ref[pl.ds(off, 2 * tq), :]], axis=0), LANES + 2 * tq,
             lambda r, c: jnp.logical_or(c < N_META, jnp.logical_and(c >= LANES, c - LANES <= r + tq)), True)

    def body(j, carry):
        off = pl.multiple_of(j * (2 * tq), 2 * tq)
        step(kb_ref[pl.ds(off, 2 * tq), :], vb_ref[pl.ds(off, 2 * tq), :], 2 * tq, None, False)
        return carry

    lax.fori_loop(0, qi // 2, body, 0)

    inv_l = 1.0 / jnp.sum(l_ref[...], axis=-1, keepdims=True)
    o = acc_ref[...] * inv_l
    lam = _diff_lambda(lam_ref, lam_init)
    o_ref[...] = _diff_finish(o[:tq], o[tq:], lam, g_ref[...], lam_init).astype(o_ref.dtype)


def _prompt_attn(proj, kmeta, vmeta, lam, subln_g, l, lam_init, tq=512):
    nq = SEQ // tq
    kcol0 = DIFF_WIDTH // DIFF_VDIM
    return pl.pallas_call(
        functools.partial(_prompt_attn_kernel, tq=tq, lam_init=lam_init),
        grid=(BATCH, DIFF_HEADS, nq),
        in_specs=[
            pl.BlockSpec((tq, DIFF_VDIM), lambda b, h, i: (b * nq + i, h)),
            pl.BlockSpec((SEQ, DIFF_VDIM), lambda b, h, i: (b, kcol0 + h)),
            pl.BlockSpec((SEQ, DIFF_VDIM), lambda b, h, i: (b, 2 * kcol0 + h)),
            pl.BlockSpec((LANES, DIFF_VDIM), lambda b, h, i: (0, h)),
            pl.BlockSpec((LANES, DIFF_VDIM), lambda b, h, i: (0, h)),
            pl.BlockSpec((None, 4, HEAD_DIM), lambda b, h, i: (l, 0, 0)),
            pl.BlockSpec((None, 1, DIFF_VDIM), lambda b, h, i: (l, 0, 0)),
        ],
        out_specs=pl.BlockSpec((tq, DIFF_VDIM), lambda b, h, i: (b * nq + i, h)),
        out_shape=jax.ShapeDtypeStruct((N_ROWS, MIX_WIDTH), BF16),
        scratch_shapes=[
            pltpu.VMEM((SEQ, DIFF_VDIM), BF16),
            pltpu.VMEM((SEQ, DIFF_VDIM), BF16),
            pltpu.VMEM((2 * tq, 2 * tq + LANES), F32),
            pltpu.VMEM((2 * tq, 2 * tq + LANES), BF16),
            pltpu.VMEM((2 * tq, LANES), F32),
            pltpu.VMEM((2 * tq, LANES), F32),
            pltpu.VMEM((2 * tq, LANES), F32),
            pltpu.VMEM((2 * tq, DIFF_VDIM), F32),
        ],
        compiler_params=_cparams(("parallel", "parallel", "arbitrary")),
        name="prompt_attn",
    )(proj, proj, proj, kmeta, vmeta, lam, subln_g)


def _meta_attn_kernel(q_ref, km_ref, vm_ref, lam_ref, g_ref, o_ref, *, lam_init):
    t = N_META
    lam = _diff_lambda(lam_ref, lam_init)
    row = lax.broadcasted_iota(jnp.int32, (2 * t, LANES), 0)
    row = jnp.where(row >= t, row - t, row)
    col = lax.broadcasted_iota(jnp.int32, (2 * t, LANES), 1)
    for h in range(DIFF_HEADS):
        sl = slice(h * DIFF_VDIM, (h + 1) * DIFF_VDIM)
        qbd = _block_diag_q(q_ref[:, sl], t).astype(BF16)
        s = lax.dot_general(qbd, km_ref[:, sl].astype(BF16), _NT, preferred_element_type=F32)
        s = jnp.where(col <= row, s, NEG)
        p = jnp.exp(s - jnp.max(s, axis=-1, keepdims=True))
        o = jnp.dot(p.astype(BF16), vm_ref[:, sl].astype(BF16), preferred_element_type=F32)
        o = o / jnp.sum(p, axis=-1, keepdims=True)
        o_ref[:, sl] = _diff_finish(o[:t], o[t:], lam, g_ref[...], lam_init)


def _meta_attn(qmeta, kmeta, vmeta, lam, subln_g, l, lam_init):
    return pl.pallas_call(
        functools.partial(_meta_attn_kernel, lam_init=lam_init),
        grid=(1,),
        in_specs=[
            pl.BlockSpec((N_META, DIFF_WIDTH), lambda i: (0, 0)),
            pl.BlockSpec((LANES, DIFF_WIDTH), lambda i: (0, 0)),
            pl.BlockSpec((LANES, DIFF_WIDTH), lambda i: (0, 0)),
            pl.BlockSpec((None, 4, HEAD_DIM), lambda i: (l, 0, 0)),
            pl.BlockSpec((None, 1, DIFF_VDIM), lambda i: (l, 0, 0)),
        ],
        out_specs=pl.BlockSpec((N_META, DIFF_WIDTH), lambda i: (0, 0)),
        out_shape=jax.ShapeDtypeStruct((N_META, DIFF_WIDTH), F32),
        compiler_params=_cparams(("arbitrary",)),
        name="meta_attn",
    )(qmeta, kmeta, vmeta, lam, subln_g)


def _by_subhead(x):
    return jnp.concatenate([x[:, h * DIFF_VDIM + c * HEAD_DIM:h * DIFF_VDIM + (c + 1) * HEAD_DIM]
                            for c in range(2) for h in range(DIFF_HEADS)], axis=0)


def _sample_attn_kernel(pt_ref, q_ref, kn_ref, vn_ref, *rest, pp, lam_init):
    kp_refs = rest[:pp]
    vp_refs = rest[pp:2 * pp]
    lam_ref, g_ref, o_ref, qs_ref, m_ref, l_ref, acc_ref = rest[2 * pp:]
    t = DEC_SEQ
    nq = NSUB * t
    g = pl.program_id(1)

    @pl.when(g == 0)
    def _():
        qs_ref[...] = _by_subhead(q_ref[...]).astype(BF16)
        m_ref[...] = jnp.full_like(m_ref, NEG)
        l_ref[...] = jnp.zeros_like(l_ref)
        acc_ref[...] = jnp.zeros_like(acc_ref)

    def attend(k_blocks, v_blocks, bias, dist):
        s = []
        for kb in k_blocks:
            sk = lax.dot_general(qs_ref[...], kb, _NT, preferred_element_type=F32)
            s.append([sk[:, i * LANES:(i + 1) * LANES] + bias for i in range(sk.shape[1] // LANES)])
        m_prev = m_ref[...]
        m_tile = None
        for sk in s:
            for st in sk:
                m_tile = st if m_tile is None else jnp.maximum(m_tile, st)
        m_new = jnp.maximum(m_prev, jnp.max(m_tile, axis=-1, keepdims=True))
        alpha = jnp.exp(m_prev - m_new)
        lane = lax.broadcasted_iota(jnp.int32, (nq, LANES), 1)
        low = (lane & dist) == 0
        l_tile = None
        pv = None
        for sk, vb in zip(s, v_blocks):
            p2 = []
            for st in sk:
                p = jnp.exp(st - m_new)
                l_tile = p if l_tile is None else l_tile + p
                sw = jnp.where(low, pltpu.roll(p, LANES - dist, 1), pltpu.roll(p, dist, 1))
                p2.append(jnp.concatenate([p, sw], axis=0).astype(BF16))
            d = jnp.dot(jnp.concatenate(p2, axis=1), vb, preferred_element_type=F32)
            pv = d if pv is None else pv + d
        acc_ref[...] = jnp.concatenate([alpha, alpha], axis=0) * acc_ref[...] + pv
        l_ref[...] = alpha * l_ref[...] + jnp.sum(l_tile, axis=-1, keepdims=True)
        m_ref[...] = m_new

    row = lax.broadcasted_iota(jnp.int32, (nq, LANES), 0)
    col = lax.broadcasted_iota(jnp.int32, (nq, LANES), 1)
    bias = jnp.where((col & (NSUB - 1)) == row // t, 0.0, NEG)
    attend([r[...].astype(BF16) for r in kp_refs], [r[...].astype(BF16) for r in vp_refs], bias, DIFF_HEADS)

    @pl.when(g == pl.num_programs(1) - 1)
    def _():
        zpad = jnp.zeros((LANES - nq, HEAD_DIM), F32)
        kn = jnp.concatenate([_by_subhead(kn_ref[...]), zpad], axis=0).astype(BF16)
        vn = jnp.concatenate([_by_subhead(vn_ref[...]), zpad], axis=0).astype(BF16)
        r = lax.broadcasted_iota(jnp.int32, (nq, LANES), 0)
        c = lax.broadcasted_iota(jnp.int32, (nq, LANES), 1)
        valid = jnp.logical_and(c // t == r // t, c % t <= r % t)
        attend([kn], [vn], jnp.where(valid, 0.0, NEG), DIFF_HEADS * t)
        lam = _diff_lambda(lam_ref, lam_init)
        o = acc_ref[...] / jnp.concatenate([l_ref[...], l_ref[...]], axis=0)
        for h in range(DIFF_HEADS):
            r1 = h * t
            r2 = (DIFF_HEADS + h) * t
            o1 = jnp.concatenate([o[r1:r1 + t], o[nq + r1:nq + r1 + t]], axis=1)
            o2 = jnp.concatenate([o[nq + r2:nq + r2 + t], o[r2:r2 + t]], axis=1)
            o_ref[:, h * DIFF_VDIM:(h + 1) * DIFF_VDIM] = _diff_finish(o1, o2, lam, g_ref[...], lam_init)


def _sample_attn(page_table, proj, cache_k, cache_v, lam, subln_g, l, lam_init, pp=16):
    n_pages = page_table.shape[1]
    row0 = ROW_SAMPLE0 // DEC_SEQ
    nq = NSUB * DEC_SEQ
    qkv_spec = lambda c: pl.BlockSpec((DEC_SEQ, DIFF_WIDTH), lambda b, g, pt: (row0 + b, c))
    page_spec = lambda i: pl.BlockSpec((None, None, PAGE_ROWS, HEAD_DIM),
                                       lambda b, g, pt: (l, pt[b, g * pp + i], 0, 0))
    grid_spec = pltpu.PrefetchScalarGridSpec(
        num_scalar_prefetch=1,
        grid=(DEC_BATCH, n_pages // pp),
        in_specs=[qkv_spec(0), qkv_spec(1), qkv_spec(2)]
        + [page_spec(i) for i in range(pp)] + [page_spec(i) for i in range(pp)]
        + [pl.BlockSpec((None, 4, HEAD_DIM), lambda b, g, pt: (l, 0, 0)),
           pl.BlockSpec((None, 1, DIFF_VDIM), lambda b, g, pt: (l, 0, 0))],
        out_specs=pl.BlockSpec((DEC_SEQ, DIFF_WIDTH), lambda b, g, pt: (b, 0)),
        scratch_shapes=[
            pltpu.VMEM((nq, HEAD_DIM), BF16),
            pltpu.VMEM((nq, 1), F32),
            pltpu.VMEM((nq, 1), F32),
            pltpu.VMEM((2 * nq, HEAD_DIM), F32),
        ],
    )
    return pl.pallas_call(
        functools.partial(_sample_attn_kernel, pp=pp, lam_init=lam_init),
        grid_spec=grid_spec,
        out_shape=jax.ShapeDtypeStruct((ROWS_SAMPLE, DIFF_WIDTH), F32),
        compiler_params=_cparams(("parallel", "arbitrary")),
        name="sample_attn",
    )(page_table, proj, proj, proj, *([cache_k] * pp), *([cache_v] * pp), lam, subln_g)


def _page_view(cache):
    d, n = cache.shape[:2]
    c = cache.reshape(d, n, PAGE_SIZE, DIFF_HEADS, 2, HEAD_DIM)
    return jnp.transpose(c, (0, 1, 2, 4, 3, 5)).reshape(d, n, PAGE_ROWS, HEAD_DIM)


def _dn_kernel(xq_ref, xk_ref, xv_ref, z_ref, ba_ref, wq_ref, wk_ref, wv_ref, alog_ref, dt_ref, ng_ref,
               sinit_ref, cinit_ref, *rest, nv, hg, into_mix):
    o_ref, sout_ref, cout_ref, xp_ref, s_ref = rest[1:] if into_mix else rest
    c = pl.program_id(2)
    last = c == pl.num_programs(2) - 1
    width = hg * DN_DK
    x_refs = (xq_ref, xk_ref, xv_ref)
    w_refs = (wq_ref, wk_ref, wv_ref)
    heads = range(hg)

    @pl.when(c == 0)
    def _():
        s_ref[...] = sinit_ref[...]
        for p in range(3):
            xp_ref[p, 0:8, :] = cinit_ref[:, p * width:(p + 1) * width]

    ys = []
    for p in range(3):
        xp_ref[p, 8:8 + nv, :] = x_refs[p][...]
        if nv < CHUNK:
            xp_ref[p, 8 + nv:, :] = jnp.zeros((CHUNK - nv, width), F32)
        xpv = xp_ref[p]
        y = xpv[8:, :] * w_refs[p][CONV_K - 1:CONV_K, :]
        for d in range(1, CONV_K):
            y = y + pltpu.roll(xpv, d, 0)[8:, :] * w_refs[p][CONV_K - 1 - d:CONV_K - d, :]
        ys.append(y * jax.nn.sigmoid(y))

    @pl.when(last)
    def _():
        for p in range(3):
            cout_ref[:, p * width:(p + 1) * width] = xp_ref[p, nv:nv + 8, :]

    if nv == CHUNK:
        for p in range(3):
            xp_ref[p, 0:8, :] = xp_ref[p, CHUNK:CHUNK + 8, :]

    ba = ba_ref[...]
    if nv < CHUNK:
        ba = jnp.concatenate([ba, jnp.zeros((CHUNK - nv, LANES), F32)], axis=0)
    gates = ba.T[0:2 * DN_HEADS, :]
    valid = lax.broadcasted_iota(jnp.int32, gates.shape, 1) < nv
    is_beta = lax.broadcasted_iota(jnp.int32, gates.shape, 0) < DN_HEADS
    ab = gates + dt_ref[...]
    softplus = jnp.maximum(ab, 0.0) + jnp.log1p(jnp.exp(-jnp.abs(ab)))
    g_t = jnp.where(jnp.logical_and(valid, jnp.logical_not(is_beta)), -jnp.exp(alog_ref[...]) * softplus, 0.0)
    ri = lax.broadcasted_iota(jnp.int32, (CHUNK, CHUNK), 0)
    ci = lax.broadcasted_iota(jnp.int32, (CHUNK, CHUNK), 1)
    causal = ri >= ci
    eye = (ri == ci).astype(F32)
    cum_t = jnp.dot(g_t, (ri <= ci).astype(F32), preferred_element_type=F32, precision=lax.Precision.HIGHEST)
    slab = jnp.where(is_beta, jnp.where(valid, jax.nn.sigmoid(gates), 0.0), cum_t)
    cols = jnp.concatenate([slab, jnp.zeros((CHUNK - 2 * DN_HEADS, CHUNK), F32)], axis=0).T

    z = z_ref[...]
    if nv < CHUNK:
        z = jnp.concatenate([z, jnp.zeros((CHUNK - nv, width), F32)], axis=0)

    sl = [slice(j * DN_DK, (j + 1) * DN_DK) for j in heads]
    beta = [cols[:, j:j + 1] for j in heads]
    gc = [cols[:, DN_HEADS + j:DN_HEADS + j + 1] for j in heads]
    gr = [cum_t[DN_HEADS + j:DN_HEADS + j + 1, :] for j in heads]
    g_last = [gc[j][CHUNK - 1:CHUNK, :] for j in heads]

    def l2n(x, scale):
        return x * (lax.rsqrt(jnp.sum(x * x, axis=-1, keepdims=True) + EPS) * scale)

    q = [l2n(ys[0][:, sl[j]], DN_DK ** -0.5) for j in heads]
    k = [l2n(ys[1][:, sl[j]], 1.0) for j in heads]
    v = [ys[2][:, sl[j]] for j in heads]
    k_t = [k[j].T for j in heads]
    s_old = [s_ref[j] for j in heads]
    qk16 = [jnp.concatenate([q[j], k[j]], axis=0).astype(BF16) for j in heads]
    a1 = [jnp.dot(qk16[j], k_t[j].astype(BF16), preferred_element_type=F32) for j in heads]
    a2 = [jnp.dot(qk16[j], s_old[j].astype(BF16), preferred_element_type=F32) for j in heads]
    decay = [jnp.exp(jnp.where(causal, gc[j] - gr[j], NEG)) for j in heads]
    e_g = [jnp.exp(gc[j]) for j in heads]
    rhs = [(beta[j] * (v[j] - e_g[j] * a2[j][CHUNK:])).astype(BF16) for j in heads]
    xpow = [a1[j][CHUNK:] * ((decay[j] - eye) * (-beta[j])) for j in heads]
    t_inv = [eye + xpow[j] for j in heads]
    xb = [xpow[j].astype(BF16) for j in heads]
    for _ in range(int(math.log2(CHUNK)) - 1):
        xb = [jnp.dot(xb[j], xb[j], preferred_element_type=F32).astype(BF16) for j in heads]
        t_inv = [t_inv[j] + jnp.dot(t_inv[j].astype(BF16), xb[j], preferred_element_type=F32) for j in heads]
    u = [jnp.dot(t_inv[j].astype(BF16), rhs[j], preferred_element_type=F32).astype(BF16) for j in heads]
    o = [e_g[j] * a2[j][:CHUNK] + jnp.dot((a1[j][:CHUNK] * decay[j]).astype(BF16), u[j],
                                          preferred_element_type=F32) for j in heads]
    for j in heads:
        tail = jnp.exp(g_last[j] - gr[j])
        s_ref[j] = jnp.exp(g_last[j]) * s_old[j] + jnp.dot((k_t[j] * tail).astype(BF16), u[j],
                                                           preferred_element_type=F32)
    for j in heads:
        zj = z[:, sl[j]]
        on = _rms(o[j], ng_ref[...]) * (zj * jax.nn.sigmoid(zj))
        o_ref[:, sl[j]] = on[:nv].astype(o_ref.dtype)

    @pl.when(last)
    def _():
        sout_ref[...] = s_ref[...]


def _dn(proj, tail, conv_w, alog_rows, dt_rows, ng, s_init, c_init, l, *, nv, row0, nseq, nchunk, shared_init,
        out_dtype=None, mix=None):
    hg = DN_HEADS
    width = hg * DN_DK
    q0 = 3 * DIFF_WIDTH // width
    rb0 = row0 // nv
    row_map = lambda s, g, c: rb0 + s * nchunk + c
    init_map = (lambda s: 0) if shared_init else (lambda s: s)
    x_spec = lambda sec: pl.BlockSpec((nv, width), lambda s, g, c: (row_map(s, g, c), q0 + sec))
    w_spec = lambda sec: pl.BlockSpec((None, CONV_K, width), lambda s, g, c: (l, 0, sec))
    in_specs = [
        x_spec(0), x_spec(1), x_spec(2), x_spec(3),
        pl.BlockSpec((nv, LANES), lambda s, g, c: (row_map(s, g, c), 0)),
        w_spec(0), w_spec(1), w_spec(2),
        pl.BlockSpec((None, 2 * DN_HEADS, LANES), lambda s, g, c: (l, 0, 0)),
        pl.BlockSpec((None, 2 * DN_HEADS, LANES), lambda s, g, c: (l, 0, 0)),
        pl.BlockSpec((None, 1, DN_DK), lambda s, g, c: (l, 0, 0)),
        pl.BlockSpec((None, hg, DN_DK, DN_DK), lambda s, g, c: (init_map(s), 0, 0, 0)),
        pl.BlockSpec((None, 8, 3 * DN_WIDTH), lambda s, g, c: (init_map(s), 0, 0)),
    ]
    args = [proj, proj, proj, proj, tail, conv_w, conv_w, conv_w, alog_rows, dt_rows, ng, s_init, c_init]
    if mix is None:
        o_spec = pl.BlockSpec((nv, width), lambda s, g, c: (s * nchunk + c, 0))
        o_shape = jax.ShapeDtypeStruct((nseq * nchunk * nv, DN_WIDTH), out_dtype)
        aliases = {}
    else:
        o_spec = pl.BlockSpec((nv, width), lambda s, g, c: (row_map(s, g, c), DIFF_WIDTH // width))
        o_shape = jax.ShapeDtypeStruct(mix.shape, mix.dtype)
        aliases = {len(args): 0}
        in_specs.append(pl.BlockSpec(memory_space=pl.ANY))
        args.append(mix)
    return pl.pallas_call(
        functools.partial(_dn_kernel, nv=nv, hg=hg, into_mix=mix is not None),
        grid=(nseq, 1, nchunk),
        in_specs=in_specs,
        input_output_aliases=aliases,
        out_specs=[
            o_spec,
            pl.BlockSpec((None, hg, DN_DK, DN_DK), lambda s, g, c: (s, 0, 0, 0)),
            pl.BlockSpec((None, 8, 3 * DN_WIDTH), lambda s, g, c: (s, 0, 0)),
        ],
        out_shape=[
            o_shape,
            jax.ShapeDtypeStruct((nseq, DN_HEADS, DN_DK, DN_DK), F32),
            jax.ShapeDtypeStruct((nseq, 8, 3 * DN_WIDTH), F32),
        ],
        scratch_shapes=[
            pltpu.VMEM((3, CHUNK + 8, width), F32),
            pltpu.VMEM((hg, DN_DK, DN_DK), F32),
        ],
        compiler_params=_cparams(("parallel", "arbitrary", "arbitrary")),
        name=f"deltanet_{nv}",
    )(*args)


def kernel(x_prompt, x_sample, cache_k, cache_v, state_ssm, state_conv, page_table, meta_tokens, norm1_g, w_in,
           q_norm_g, k_norm_g, diff_lambda, subln_g, conv_w, a_log, dt_bias, dn_norm_g, w_out, norm2_g, w_up,
           w_down):
    cache_k = _page_view(cache_k)
    cache_v = _page_view(cache_v)

    w_main = w_in.astype(BF16)
    w_tail = jnp.pad(w_in[:, :, MAIN_COLS:], ((0, 0), (0, 0), (0, LANES - 2 * DN_HEADS))).astype(BF16)
    w_out_b = w_out.astype(BF16)
    w_up_b = w_up.astype(BF16)
    w_down_b = w_down.astype(BF16)
    vec = lambda a: a.reshape(DEPTH, 1, a.shape[-1])
    gate_rows = lambda a: jnp.broadcast_to(jnp.pad(a, ((0, 0), (DN_HEADS, 0)))[:, :, None],
                                           (DEPTH, 2 * DN_HEADS, LANES))
    alog_rows = gate_rows(a_log)
    dt_rows = gate_rows(dt_bias)
    norm1_g, norm2_g, q_norm_g, k_norm_g = vec(norm1_g), vec(norm2_g), vec(q_norm_g), vec(k_norm_g)
    subln_g, dn_norm_g = vec(subln_g), vec(dn_norm_g)
    conv_state = jnp.pad(state_conv, ((0, 0), (0, 0), (8 - (CONV_K - 1), 0), (0, 0)))

    x = jnp.concatenate([
        x_prompt.reshape(ROWS_PROMPT, D_MODEL),
        x_sample.reshape(ROWS_SAMPLE, D_MODEL),
        meta_tokens,
        jnp.zeros((N_ROWS - ROW_META0 - N_META, D_MODEL), F32),
    ], axis=0)

    zero_state = jnp.zeros((1, DN_HEADS, DN_DK, DN_DK), F32)
    zero_conv = jnp.zeros((1, 8, 3 * DN_WIDTH), F32)
    outs = [[] for _ in range(8)]
    for l in range(DEPTH):
        lam_init = 0.8 - 0.6 * math.exp(-0.3 * l)
        proj, tail, k2, v2 = _in_proj(x, norm1_g, w_main, w_tail, q_norm_g, k_norm_g, l)
        k_all = proj[:, DIFF_WIDTH:2 * DIFF_WIDTH]
        v_all = proj[:, 2 * DIFF_WIDTH:3 * DIFF_WIDTH]
        meta = slice(ROW_META0, ROW_META0 + N_META)
        kmeta = jnp.pad(k_all[meta], ((0, LANES - N_META), (0, 0)))
        vmeta = jnp.pad(v_all[meta], ((0, LANES - N_META), (0, 0)))

        att_p = _prompt_attn(proj, kmeta, vmeta, diff_lambda, subln_g, l, lam_init)
        att_m = _meta_attn(proj[meta, :DIFF_WIDTH], kmeta, vmeta, diff_lambda, subln_g, l, lam_init)
        att_s = _sample_attn(page_table, proj, cache_k, cache_v, diff_lambda, subln_g, l, lam_init)

        dn_args = (proj, tail, conv_w, alog_rows, dt_rows, dn_norm_g)
        dn_m, s_meta, c_meta = _dn(*dn_args, zero_state, zero_conv, l, nv=N_META, row0=ROW_META0, nseq=1, nchunk=1,
                                   shared_init=True, out_dtype=F32)
        dn_s, s_samp, c_samp = _dn(*dn_args, state_ssm[l], conv_state[l], l, nv=DEC_SEQ, row0=ROW_SAMPLE0,
                                   nseq=DEC_BATCH, nchunk=1, shared_init=False, out_dtype=F32)
        mix, s_prom, c_prom = _dn(*dn_args, s_meta, c_meta, l, nv=CHUNK, row0=0, nseq=BATCH, nchunk=SEQ // CHUNK,
                                  shared_init=True, mix=att_p)
        mix_extra = jnp.concatenate([
            jnp.concatenate([att_s, dn_s], axis=1),
            jnp.concatenate([att_m, dn_m], axis=1),
            jnp.zeros((N_ROWS - ROW_META0 - N_META, MIX_WIDTH), F32),
        ], axis=0).astype(BF16)
        mix = lax.dynamic_update_slice(mix, mix_extra, (ROW_SAMPLE0, 0))
        x = _out_proj(mix, w_out_b, x, l)
        x = _ffn(x, norm2_g, w_up_b, w_down_b, l)

        samp = slice(ROW_SAMPLE0, ROW_SAMPLE0 + ROWS_SAMPLE)
        for t2, prompt_rows, sample_rows in ((k2, outs[0], outs[2]), (v2, outs[1], outs[3])):
            for b in range(BATCH):
                prompt_rows.append(t2[meta.start * NSUB:meta.stop * NSUB])
                prompt_rows.append(t2[b * SEQ * NSUB:(b + 1) * SEQ * NSUB])
            sample_rows.append(t2[samp.start * NSUB:samp.stop * NSUB])
        outs[4].append(s_prom)
        outs[5].append(s_samp)
        outs[6].append(c_prom[:, 8 - (CONV_K - 1):])
        outs[7].append(c_samp[:, 8 - (CONV_K - 1):])

    y_prompt = x[:ROWS_PROMPT].reshape(BATCH, SEQ, D_MODEL)
    y_sample = x[ROW_SAMPLE0:ROW_SAMPLE0 + ROWS_SAMPLE].reshape(DEC_BATCH, DEC_SEQ, D_MODEL)

    def unpaged(rows, lead):
        t = jnp.concatenate(rows, axis=0).reshape(*lead, 2, DIFF_HEADS, HEAD_DIM)
        t = jnp.moveaxis(t, -3, -2)
        return t.reshape(*lead, DIFF_HEADS, DIFF_VDIM)

    kv = [unpaged(outs[i], (DEPTH, BATCH, N_META + SEQ)) for i in (0, 1)]
    kv += [unpaged(outs[i], (DEPTH, DEC_BATCH, DEC_SEQ)) for i in (2, 3)]
    return (y_prompt, y_sample) + tuple(kv) + tuple(jnp.stack(o) for o in outs[4:])
```

```python
import functools
import math

import jax
import jax.numpy as jnp
from jax import lax
from jax.experimental import pallas as pl
from jax.experimental.pallas import tpu as pltpu

F32 = jnp.float32
BF16 = jnp.bfloat16

D_MODEL = 2048
BATCH = 4
SEQ = 2048
DEPTH = 4
DEC_BATCH = 8
DEC_SEQ = 8
PAGE_SIZE = 128
N_META = 16
HEAD_DIM = 128
DIFF_HEADS = 4
DIFF_VDIM = 2 * HEAD_DIM
DIFF_WIDTH = DIFF_HEADS * DIFF_VDIM
DN_HEADS = 8
DN_DK = 128
DN_WIDTH = DN_HEADS * DN_DK
MIX_WIDTH = DIFF_WIDTH + DN_WIDTH
CONV_K = 4
D_FF = 4 * D_MODEL
EPS = 1e-6
MAIN_COLS = 3 * DIFF_WIDTH + 4 * DN_WIDTH

ROWS_PROMPT = BATCH * SEQ
ROWS_SAMPLE = DEC_BATCH * DEC_SEQ
ROW_SAMPLE0 = ROWS_PROMPT
ROW_META0 = ROWS_PROMPT + ROWS_SAMPLE
N_ROWS = 8320
TM = 640
TM_IN = 1040
LANES = 128
CHUNK = 128
NEG = -1e30
PAGE_ROWS = PAGE_SIZE * 2 * DIFF_HEADS
NSUB = 2 * DIFF_HEADS
ROW_CHUNK = 32
VMEM_LIMIT = 56 * 1024 * 1024


def _cparams(sem):
    return pltpu.CompilerParams(dimension_semantics=sem, vmem_limit_bytes=VMEM_LIMIT)


def _silu(x):
    h = 0.5 * x
    return h + h * jnp.tanh(h)


def _rms(x, gain):
    return x * lax.rsqrt(jnp.mean(x * x, axis=-1, keepdims=True) + EPS) * gain


def _store_paged(dst_ref, cols, tile):
    n_rows = cols.shape[0]
    for j in range(cols.shape[1] // HEAD_DIM):
        piece = tile * (cols.shape[1] // HEAD_DIM) + j
        dst_ref[pl.ds((piece % 2) * DIFF_HEADS + piece // 2, n_rows, stride=NSUB), :] = (
            cols[:, j * HEAD_DIM:(j + 1) * HEAD_DIM])


def _in_proj_kernel(x_ref, g_ref, w_ref, wt_ref, qg_ref, kg_ref, o_ref, ot_ref, k2_ref, v2_ref, xn_ref, *,
                    n_q, n_k):
    n = pl.program_id(1)

    @pl.when(n == 0)
    def _():
        xn_ref[...] = _rms(x_ref[...], g_ref[...]).astype(BF16)
        ot_ref[...] = jnp.dot(xn_ref[...], wt_ref[...], preferred_element_type=F32)

    o_ref[...] = jnp.dot(xn_ref[...], w_ref[...], preferred_element_type=F32)

    def head_norm(gain):
        acc = o_ref[...]
        cols = [_rms(acc[:, j * HEAD_DIM:(j + 1) * HEAD_DIM], gain) for j in range(acc.shape[1] // HEAD_DIM)]
        return jnp.concatenate(cols, axis=1)

    @pl.when(n < n_q)
    def _():
        o_ref[...] = head_norm(qg_ref[...] * (HEAD_DIM ** -0.5))

    for t in range(n_k):
        @pl.when(n == n_q + t)
        def _():
            kn = head_norm(kg_ref[...])
            o_ref[...] = kn
            _store_paged(k2_ref, kn, t)

        @pl.when(n == n_q + n_k + t)
        def _():
            _store_paged(v2_ref, o_ref[...], t)


def _in_proj(x, g, w_main, w_tail, qg, kg, l, tn=512):
    n_q = DIFF_WIDTH // tn
    return pl.pallas_call(
        functools.partial(_in_proj_kernel, n_q=n_q, n_k=n_q),
        grid=(N_ROWS // TM_IN, MAIN_COLS // tn),
        in_specs=[
            pl.BlockSpec((TM_IN, D_MODEL), lambda m, n: (m, 0)),
            pl.BlockSpec((None, 1, D_MODEL), lambda m, n: (l, 0, 0)),
            pl.BlockSpec((None, D_MODEL, tn), lambda m, n: (l, 0, n)),
            pl.BlockSpec((None, D_MODEL, LANES), lambda m, n: (l, 0, 0)),
            pl.BlockSpec((None, 1, HEAD_DIM), lambda m, n: (l, 0, 0)),
            pl.BlockSpec((None, 1, HEAD_DIM), lambda m, n: (l, 0, 0)),
        ],
        out_specs=[
            pl.BlockSpec((TM_IN, tn), lambda m, n: (m, n)),
            pl.BlockSpec((TM_IN, LANES), lambda m, n: (m, 0)),
            pl.BlockSpec((TM_IN * NSUB, HEAD_DIM), lambda m, n: (m, 0)),
            pl.BlockSpec((TM_IN * NSUB, HEAD_DIM), lambda m, n: (m, 0)),
        ],
        out_shape=[
            jax.ShapeDtypeStruct((N_ROWS, MAIN_COLS), F32),
            jax.ShapeDtypeStruct((N_ROWS, LANES), F32),
            jax.ShapeDtypeStruct((N_ROWS * NSUB, HEAD_DIM), F32),
            jax.ShapeDtypeStruct((N_ROWS * NSUB, HEAD_DIM), F32),
        ],
        scratch_shapes=[pltpu.VMEM((TM_IN, D_MODEL), BF16)],
        compiler_params=_cparams(("parallel", "arbitrary")),
        name="in_proj",
    )(x, g, w_main, w_tail, qg, kg)


def _out_proj_kernel(mix_ref, w_ref, x_ref, o_ref):
    o_ref[...] = x_ref[...] + jnp.dot(mix_ref[...], w_ref[...], preferred_element_type=F32)


def _out_proj(mix, w_out, x, l):
    k = mix.shape[1]
    return pl.pallas_call(
        _out_proj_kernel,
        grid=(N_ROWS // TM,),
        in_specs=[
            pl.BlockSpec((TM, k), lambda m: (m, 0)),
            pl.BlockSpec((None, k, D_MODEL), lambda m: (l, 0, 0)),
            pl.BlockSpec((TM, D_MODEL), lambda m: (m, 0)),
        ],
        out_specs=pl.BlockSpec((TM, D_MODEL), lambda m: (m, 0)),
        out_shape=jax.ShapeDtypeStruct((N_ROWS, D_MODEL), F32),
        compiler_params=_cparams(("parallel",)),
        name="out_proj",
    )(mix, w_out, x)


def _ffn_kernel(x_ref, g_ref, wu_ref, wd_ref, o_ref, xn_ref):
    f = pl.program_id(1)

    @pl.when(f == 0)
    def _():
        xn_ref[...] = _rms(x_ref[...], g_ref[...]).astype(BF16)
        o_ref[...] = x_ref[...]

    h = jnp.maximum(jnp.dot(xn_ref[...], wu_ref[...], preferred_element_type=F32), 0.0)
    o_ref[...] += jnp.dot((h * h).astype(BF16), wd_ref[...], preferred_element_type=F32)


def _ffn(x, g, w_up, w_down, l, tf=1024):
    return pl.pallas_call(
        _ffn_kernel,
        grid=(N_ROWS // TM, D_FF // tf),
        in_specs=[
            pl.BlockSpec((TM, D_MODEL), lambda m, f: (m, 0)),
            pl.BlockSpec((None, 1, D_MODEL), lambda m, f: (l, 0, 0)),
            pl.BlockSpec((None, D_MODEL, tf), lambda m, f: (l, 0, f)),
            pl.BlockSpec((None, tf, D_MODEL), lambda m, f: (l, f, 0)),
        ],
        out_specs=pl.BlockSpec((TM, D_MODEL), lambda m, f: (m, 0)),
        out_shape=jax.ShapeDtypeStruct((N_ROWS, D_MODEL), F32),
        scratch_shapes=[pltpu.VMEM((TM, D_MODEL), BF16)],
        compiler_params=_cparams(("parallel", "arbitrary")),
        name="ffn",
    )(x, g, w_up, w_down)


def _diff_lambda(lam_ref, lam_init):
    lf = lam_ref[...]
    s1 = jnp.sum(lf[0:1] * lf[1:2], axis=-1, keepdims=True)
    s2 = jnp.sum(lf[2:3] * lf[3:4], axis=-1, keepdims=True)
    return jnp.exp(s1) - jnp.exp(s2) + lam_init


def _diff_finish(o1, o2, lam, g, lam_init):
    return _rms(o1 - lam * o2, g) * (1.0 - lam_init)


def _block_diag_q(q, t):
    z = jnp.zeros((t, HEAD_DIM), q.dtype)
    top = jnp.concatenate([q[:, :HEAD_DIM], z], axis=1)
    bot = jnp.concatenate([z, q[:, HEAD_DIM:]], axis=1)
    return jnp.concatenate([top, bot], axis=0)


_NT = (((1,), (1,)), ((), ()))


def _prompt_attn_kernel(q_ref, k_ref, v_ref, km_ref, vm_ref, lam_ref, g_ref, o_ref,
                        kb_ref, vb_ref, s_ref, p_ref, m_ref, l_ref, a_ref, acc_ref, *, tq, lam_init):
    qi = pl.program_id(2)
    rows = 2 * tq

    @pl.when(qi == 0)
    def _():
        kb_ref[...] = k_ref[...].astype(BF16)
        vb_ref[...] = v_ref[...].astype(BF16)

    q = q_ref[...].astype(BF16)
    q1 = q[:, :HEAD_DIM]
    q2 = q[:, HEAD_DIM:]

    def step(kblk, vblk, nk, visible, first):
        s_ref[0:tq, 0:nk] = lax.dot_general(q1, kblk[:, :HEAD_DIM], _NT, preferred_element_type=F32)
        s_ref[tq:rows, 0:nk] = lax.dot_general(q2, kblk[:, HEAD_DIM:], _NT, preferred_element_type=F32)
        reps = nk // LANES
        for rc in range(rows // ROW_CHUNK):
            rs = slice(rc * ROW_CHUNK, (rc + 1) * ROW_CHUNK)
            s = s_ref[rs, 0:nk]
            if visible is not None:
                r = lax.broadcasted_iota(jnp.int32, (ROW_CHUNK, nk), 0) + (rc * ROW_CHUNK) % tq
                c = lax.broadcasted_iota(jnp.int32, (ROW_CHUNK, nk), 1)
                s = jnp.where(visible(r, c), s, NEG)
            m_new = jnp.broadcast_to(jnp.max(s, axis=-1, keepdims=True), (ROW_CHUNK, LANES))
            if not first:
                m_prev = m_ref[rs, :]
                m_new = jnp.maximum(m_prev, m_new)
                alpha = jnp.exp(m_prev - m_new)
                a_ref[rs, :] = alpha
            m_ref[rs, :] = m_new
            p = jnp.exp(s - jnp.concatenate([m_new] * reps, axis=1))
            p_ref[rs, 0:nk] = p.astype(BF16)
            psum = p[:, 0:LANES]
            for i in range(1, reps):
                psum = psum + p[:, i * LANES:(i + 1) * LANES]
            l_ref[rs, :] = psum if first else alpha * l_ref[rs, :] + psum
        pv = jnp.dot(p_ref[:, 0:nk], vblk, preferred_element_type=F32)
        if first:
            acc_ref[...] = pv
        else:
            alpha = a_ref[...]
            acc_ref[...] = jnp.concatenate([alpha, alpha], axis=1) * acc_ref[...] + pv

    km = km_ref[...].astype(BF16)
    vm = vm_ref[...].astype(BF16)

    @pl.when(qi % 2 == 0)
    def _():
        off = pl.multiple_of(qi * tq, tq)
        step(jnp.concatenate([km, kb_ref[pl.ds(off, tq), :]], axis=0),
             jnp.concatenate([vm, vb_ref[pl.ds(off, tq), :]], axis=0), LANES + tq,
             lambda r, c: jnp.logical_or(c < N_META, jnp.logical_and(c >= LANES, c - LANES <= r)), True)

    @pl.when(qi % 2 == 1)
    def _():
        off = pl.multiple_of((qi - 1) * tq, tq)
        step(jnp.concatenate([km, kb_ref[pl.ds(off, 2 * tq), :]], axis=0),
             jnp.concatenate([vm, vb_ref[pl.ds(off, 2 * tq), :]], axis=0), LANES + 2 * tq,
             lambda r, c: jnp.logical_or(c < N_META, jnp.logical_and(c >= LANES, c - LANES <= r + tq)), True)

    def body(j, carry):
        off = pl.multiple_of(j * (2 * tq), 2 * tq)
        step(kb_ref[pl.ds(off, 2 * tq), :], vb_ref[pl.ds(off, 2 * tq), :], 2 * tq, None, False)
        return carry

    lax.fori_loop(0, qi // 2, body, 0)

    inv_l = 1.0 / jnp.sum(l_ref[...], axis=-1, keepdims=True)
    o = acc_ref[...] * inv_l
    lam = _diff_lambda(lam_ref, lam_init)
    o_ref[...] = _diff_finish(o[:tq], o[tq:], lam, g_ref[...], lam_init).astype(o_ref.dtype)


def _prompt_attn(proj, kmeta, vmeta, lam, subln_g, l, lam_init, tq=512):
    nq = SEQ // tq
    kcol0 = DIFF_WIDTH // DIFF_VDIM
    return pl.pallas_call(
        functools.partial(_prompt_attn_kernel, tq=tq, lam_init=lam_init),
        grid=(BATCH, DIFF_HEADS, nq),
        in_specs=[
            pl.BlockSpec((tq, DIFF_VDIM), lambda b, h, i: (b * nq + i, h)),
            pl.BlockSpec((SEQ, DIFF_VDIM), lambda b, h, i: (b, kcol0 + h)),
            pl.BlockSpec((SEQ, DIFF_VDIM), lambda b, h, i: (b, 2 * kcol0 + h)),
            pl.BlockSpec((LANES, DIFF_VDIM), lambda b, h, i: (0, h)),
            pl.BlockSpec((LANES, DIFF_VDIM), lambda b, h, i: (0, h)),
            pl.BlockSpec((None, 4, HEAD_DIM), lambda b, h, i: (l, 0, 0)),
            pl.BlockSpec((None, 1, DIFF_VDIM), lambda b, h, i: (l, 0, 0)),
        ],
        out_specs=pl.BlockSpec((tq, DIFF_VDIM), lambda b, h, i: (b * nq + i, h)),
        out_shape=jax.ShapeDtypeStruct((N_ROWS, MIX_WIDTH), BF16),
        scratch_shapes=[
            pltpu.VMEM((SEQ, DIFF_VDIM), BF16),
            pltpu.VMEM((SEQ, DIFF_VDIM), BF16),
            pltpu.VMEM((2 * tq, 2 * tq + LANES), F32),
            pltpu.VMEM((2 * tq, 2 * tq + LANES), BF16),
            pltpu.VMEM((2 * tq, LANES), F32),
            pltpu.VMEM((2 * tq, LANES), F32),
            pltpu.VMEM((2 * tq, LANES), F32),
            pltpu.VMEM((2 * tq, DIFF_VDIM), F32),
        ],
        compiler_params=_cparams(("parallel", "parallel", "arbitrary")),
        name="prompt_attn",
    )(proj, proj, proj, kmeta, vmeta, lam, subln_g)


def _meta_attn_kernel(q_ref, km_ref, vm_ref, lam_ref, g_ref, o_ref, *, lam_init):
    t = N_META
    lam = _diff_lambda(lam_ref, lam_init)
    row = lax.broadcasted_iota(jnp.int32, (2 * t, LANES), 0)
    row = jnp.where(row >= t, row - t, row)
    col = lax.broadcasted_iota(jnp.int32, (2 * t, LANES), 1)
    for h in range(DIFF_HEADS):
        sl = slice(h * DIFF_VDIM, (h + 1) * DIFF_VDIM)
        qbd = _block_diag_q(q_ref[:, sl], t).astype(BF16)
        s = lax.dot_general(qbd, km_ref[:, sl].astype(BF16), _NT, preferred_element_type=F32)
        s = jnp.where(col <= row, s, NEG)
        p = jnp.exp(s - jnp.max(s, axis=-1, keepdims=True))
        o = jnp.dot(p.astype(BF16), vm_ref[:, sl].astype(BF16), preferred_element_type=F32)
        o = o / jnp.sum(p, axis=-1, keepdims=True)
        o_ref[:, sl] = _diff_finish(o[:t], o[t:], lam, g_ref[...], lam_init)


def _meta_attn(qmeta, kmeta, vmeta, lam, subln_g, l, lam_init):
    return pl.pallas_call(
        functools.partial(_meta_attn_kernel, lam_init=lam_init),
        grid=(1,),
        in_specs=[
            pl.BlockSpec((N_META, DIFF_WIDTH), lambda i: (0, 0)),
            pl.BlockSpec((LANES, DIFF_WIDTH), lambda i: (0, 0)),
            pl.BlockSpec((LANES, DIFF_WIDTH), lambda i: (0, 0)),
            pl.BlockSpec((None, 4, HEAD_DIM), lambda i: (l, 0, 0)),
            pl.BlockSpec((None, 1, DIFF_VDIM), lambda i: (l, 0, 0)),
        ],
        out_specs=pl.BlockSpec((N_META, DIFF_WIDTH), lambda i: (0, 0)),
        out_shape=jax.ShapeDtypeStruct((N_META, DIFF_WIDTH), F32),
        compiler_params=_cparams(("arbitrary",)),
        name="meta_attn",
    )(qmeta, kmeta, vmeta, lam, subln_g)


def _sample_attn_kernel(pt_ref, q_ref, kn_ref, vn_ref, *rest, pp, lam_init):
    kp_refs = rest[:pp]
    vp_refs = rest[pp:2 * pp]
    lam_ref, g_ref, o_ref, qbd_ref, m_ref, l_ref, acc_ref = rest[2 * pp:]
    t = DEC_SEQ
    rows_h = 2 * t
    g = pl.program_id(1)

    @pl.when(g == 0)
    def _():
        for h in range(DIFF_HEADS):
            qbd_ref[h] = _block_diag_q(q_ref[:, h * DIFF_VDIM:(h + 1) * DIFF_VDIM], t).astype(BF16)
        m_ref[...] = jnp.full_like(m_ref, NEG)
        l_ref[...] = jnp.zeros_like(l_ref)
        acc_ref[...] = jnp.zeros_like(acc_ref)

    def attend(kv_blocks, bias):
        s = [jnp.concatenate([lax.dot_general(qbd_ref[h], k[h], _NT, preferred_element_type=F32)
                              for h in range(DIFF_HEADS)], axis=0) for k, _ in kv_blocks]
        if bias is not None:
            s = [sb + bias for sb in s]
        m_prev = m_ref[...]
        m_tile = s[0]
        for sb in s[1:]:
            m_tile = jnp.maximum(m_tile, sb)
        m_new = jnp.maximum(m_prev, jnp.max(m_tile, axis=-1, keepdims=True))
        alpha = jnp.exp(m_prev - m_new)
        p = [jnp.exp(sb - m_new) for sb in s]
        l_tile = p[0]
        for pb in p[1:]:
            l_tile = l_tile + pb
        l_ref[...] = alpha * l_ref[...] + jnp.sum(l_tile, axis=-1, keepdims=True)
        m_ref[...] = m_new
        for h in range(DIFF_HEADS):
            rs = slice(h * rows_h, (h + 1) * rows_h)
            pv = None
            for pb, (_, v) in zip(p, kv_blocks):
                d = jnp.dot(pb[rs].astype(BF16), v[h], preferred_element_type=F32)
                pv = d if pv is None else pv + d
            acc_ref[rs, :] = alpha[rs] * acc_ref[rs, :] + pv

    def page_heads(ref):
        return [jnp.concatenate([ref[pl.ds(c * DIFF_HEADS + h, PAGE_SIZE, stride=NSUB), :] for c in range(2)],
                                axis=1).astype(BF16) for h in range(DIFF_HEADS)]

    attend([(page_heads(kr), page_heads(vr)) for kr, vr in zip(kp_refs, vp_refs)], None)

    @pl.when(g == pl.num_programs(1) - 1)
    def _():
        zpad = jnp.zeros((LANES - t, DIFF_WIDTH), F32)
        kn = jnp.concatenate([kn_ref[...], zpad], axis=0).astype(BF16)
        vn = jnp.concatenate([vn_ref[...], zpad], axis=0).astype(BF16)
        heads = lambda x: [x[:, h * DIFF_VDIM:(h + 1) * DIFF_VDIM] for h in range(DIFF_HEADS)]
        r = lax.broadcasted_iota(jnp.int32, (DIFF_HEADS * rows_h, LANES), 0)
        c = lax.broadcasted_iota(jnp.int32, (DIFF_HEADS * rows_h, LANES), 1)
        attend([(heads(kn), heads(vn))], jnp.where(c <= r % t, 0.0, NEG))
        lam = _diff_lambda(lam_ref, lam_init)
        o = acc_ref[...] / l_ref[...]
        for h in range(DIFF_HEADS):
            o_ref[:, h * DIFF_VDIM:(h + 1) * DIFF_VDIM] = _diff_finish(
                o[h * rows_h:h * rows_h + t], o[h * rows_h + t:(h + 1) * rows_h], lam, g_ref[...], lam_init)


def _sample_attn(page_table, proj, cache_k, cache_v, lam, subln_g, l, lam_init, pp=16):
    n_pages = page_table.shape[1]
    row0 = ROW_SAMPLE0 // DEC_SEQ
    nq = NSUB * DEC_SEQ
    qkv_spec = lambda c: pl.BlockSpec((DEC_SEQ, DIFF_WIDTH), lambda b, g, pt: (row0 + b, c))
    page_spec = lambda i: pl.BlockSpec((None, None, PAGE_ROWS, HEAD_DIM),
                                       lambda b, g, pt: (l, pt[b, g * pp + i], 0, 0))
    grid_spec = pltpu.PrefetchScalarGridSpec(
        num_scalar_prefetch=1,
        grid=(DEC_BATCH, n_pages // pp),
        in_specs=[qkv_spec(0), qkv_spec(1), qkv_spec(2)]
        + [page_spec(i) for i in range(pp)] + [page_spec(i) for i in range(pp)]
        + [pl.BlockSpec((None, 4, HEAD_DIM), lambda b, g, pt: (l, 0, 0)),
           pl.BlockSpec((None, 1, DIFF_VDIM), lambda b, g, pt: (l, 0, 0))],
        out_specs=pl.BlockSpec((DEC_SEQ, DIFF_WIDTH), lambda b, g, pt: (b, 0)),
        scratch_shapes=[
            pltpu.VMEM((DIFF_HEADS, 2 * DEC_SEQ, DIFF_VDIM), BF16),
            pltpu.VMEM((nq, 1), F32),
            pltpu.VMEM((nq, 1), F32),
            pltpu.VMEM((nq, DIFF_VDIM), F32),
        ],
    )
    return pl.pallas_call(
        functools.partial(_sample_attn_kernel, pp=pp, lam_init=lam_init),
        grid_spec=grid_spec,
        out_shape=jax.ShapeDtypeStruct((ROWS_SAMPLE, DIFF_WIDTH), F32),
        compiler_params=_cparams(("parallel", "arbitrary")),
        name="sample_attn",
    )(page_table, proj, proj, proj, *([cache_k] * pp), *([cache_v] * pp), lam, subln_g)


def _page_view(cache):
    d, n = cache.shape[:2]
    c = cache.reshape(d, n, PAGE_SIZE, DIFF_HEADS, 2, HEAD_DIM)
    return jnp.transpose(c, (0, 1, 2, 4, 3, 5)).reshape(d, n, PAGE_ROWS, HEAD_DIM)


def _dn_kernel(xq_ref, xk_ref, xv_ref, z_ref, ba_ref, wq_ref, wk_ref, wv_ref, alog_ref, dt_ref, ng_ref,
               sinit_ref, cinit_ref, *rest, nv, hg, into_mix):
    o_ref, sout_ref, cout_ref, xp_ref, s_ref = rest[1:] if into_mix else rest
    c = pl.program_id(2)
    last = c == pl.num_programs(2) - 1
    width = hg * DN_DK
    x_refs = (xq_ref, xk_ref, xv_ref)
    w_refs = (wq_ref, wk_ref, wv_ref)
    heads = range(hg)

    @pl.when(c == 0)
    def _():
        s_ref[...] = sinit_ref[...]
        for p in range(3):
            xp_ref[p, 0:8, :] = cinit_ref[:, p * width:(p + 1) * width]

    ys = []
    for p in range(3):
        xp_ref[p, 8:8 + nv, :] = x_refs[p][...]
        if nv < CHUNK:
            xp_ref[p, 8 + nv:, :] = jnp.zeros((CHUNK - nv, width), F32)
        xpv = xp_ref[p]
        y = xpv[8:, :] * w_refs[p][CONV_K - 1:CONV_K, :]
        for d in range(1, CONV_K):
            y = y + pltpu.roll(xpv, d, 0)[8:, :] * w_refs[p][CONV_K - 1 - d:CONV_K - d, :]
        ys.append(_silu(y))

    @pl.when(last)
    def _():
        for p in range(3):
            cout_ref[:, p * width:(p + 1) * width] = xp_ref[p, nv:nv + 8, :]

    if nv == CHUNK:
        for p in range(3):
            xp_ref[p, 0:8, :] = xp_ref[p, CHUNK:CHUNK + 8, :]

    ba = ba_ref[...]
    if nv < CHUNK:
        ba = jnp.concatenate([ba, jnp.zeros((CHUNK - nv, LANES), F32)], axis=0)
    gates = ba.T[0:2 * DN_HEADS, :]
    valid = lax.broadcasted_iota(jnp.int32, gates.shape, 1) < nv
    is_beta = lax.broadcasted_iota(jnp.int32, gates.shape, 0) < DN_HEADS
    ab = gates + dt_ref[...]
    softplus = jnp.maximum(ab, 0.0) + jnp.log1p(jnp.exp(-jnp.abs(ab)))
    g_t = jnp.where(jnp.logical_and(valid, jnp.logical_not(is_beta)), -jnp.exp(alog_ref[...]) * softplus, 0.0)
    ri = lax.broadcasted_iota(jnp.int32, (CHUNK, CHUNK), 0)
    ci = lax.broadcasted_iota(jnp.int32, (CHUNK, CHUNK), 1)
    causal = ri >= ci
    eye = (ri == ci).astype(F32)
    cum_t = jnp.dot(g_t, (ri <= ci).astype(F32), preferred_element_type=F32, precision=lax.Precision.HIGHEST)
    slab = jnp.where(is_beta, jnp.where(valid, jax.nn.sigmoid(gates), 0.0), cum_t)
    cols = jnp.concatenate([slab, jnp.zeros((CHUNK - 2 * DN_HEADS, CHUNK), F32)], axis=0).T

    z = z_ref[...]
    if nv < CHUNK:
        z = jnp.concatenate([z, jnp.zeros((CHUNK - nv, width), F32)], axis=0)

    sl = [slice(j * DN_DK, (j + 1) * DN_DK) for j in heads]
    beta = [cols[:, j:j + 1] for j in heads]
    gc = [cols[:, DN_HEADS + j:DN_HEADS + j + 1] for j in heads]
    gr = [cum_t[DN_HEADS + j:DN_HEADS + j + 1, :] for j in heads]
    g_last = [gc[j][CHUNK - 1:CHUNK, :] for j in heads]

    def l2n(x, scale):
        return x * (lax.rsqrt(jnp.sum(x * x, axis=-1, keepdims=True) + EPS) * scale)

    q = [l2n(ys[0][:, sl[j]], DN_DK ** -0.5) for j in heads]
    k = [l2n(ys[1][:, sl[j]], 1.0) for j in heads]
    v = [ys[2][:, sl[j]] for j in heads]
    k_t = [k[j].T for j in heads]
    s_old = [s_ref[j] for j in heads]
    qk16 = [jnp.concatenate([q[j], k[j]], axis=0).astype(BF16) for j in heads]
    a1 = [jnp.dot(qk16[j], k_t[j].astype(BF16), preferred_element_type=F32) for j in heads]
    a2 = [jnp.dot(qk16[j], s_old[j].astype(BF16), preferred_element_type=F32) for j in heads]
    decay = [jnp.exp(jnp.where(causal, gc[j] - gr[j], NEG)) for j in heads]
    e_g = [jnp.exp(gc[j]) for j in heads]
    rhs = [(beta[j] * (v[j] - e_g[j] * a2[j][CHUNK:])).astype(BF16) for j in heads]
    xpow = [a1[j][CHUNK:] * ((decay[j] - eye) * (-beta[j])) for j in heads]
    t_inv = [eye + xpow[j] for j in heads]
    xb = [xpow[j].astype(BF16) for j in heads]
    for _ in range(int(math.log2(CHUNK)) - 1):
        xb = [jnp.dot(xb[j], xb[j], preferred_element_type=F32).astype(BF16) for j in heads]
        t_inv = [t_inv[j] + jnp.dot(t_inv[j].astype(BF16), xb[j], preferred_element_type=F32) for j in heads]
    u = [jnp.dot(t_inv[j].astype(BF16), rhs[j], preferred_element_type=F32).astype(BF16) for j in heads]
    o = [e_g[j] * a2[j][:CHUNK] + jnp.dot((a1[j][:CHUNK] * decay[j]).astype(BF16), u[j],
                                          preferred_element_type=F32) for j in heads]
    for j in heads:
        tail = jnp.exp(g_last[j] - gr[j])
        s_ref[j] = jnp.exp(g_last[j]) * s_old[j] + jnp.dot((k_t[j] * tail).astype(BF16), u[j],
                                                           preferred_element_type=F32)
    for j in heads:
        zj = z[:, sl[j]]
        on = _rms(o[j], ng_ref[...]) * _silu(zj)
        o_ref[:, sl[j]] = on[:nv].astype(o_ref.dtype)

    @pl.when(last)
    def _():
        sout_ref[...] = s_ref[...]


def _dn(proj, tail, conv_w, alog_rows, dt_rows, ng, s_init, c_init, l, *, nv, row0, nseq, nchunk, shared_init,
        out_dtype=None, mix=None):
    hg = DN_HEADS
    width = hg * DN_DK
    q0 = 3 * DIFF_WIDTH // width
    rb0 = row0 // nv
    row_map = lambda s, g, c: rb0 + s * nchunk + c
    init_map = (lambda s: 0) if shared_init else (lambda s: s)
    x_spec = lambda sec: pl.BlockSpec((nv, width), lambda s, g, c: (row_map(s, g, c), q0 + sec))
    w_spec = lambda sec: pl.BlockSpec((None, CONV_K, width), lambda s, g, c: (l, 0, sec))
    in_specs = [
        x_spec(0), x_spec(1), x_spec(2), x_spec(3),
        pl.BlockSpec((nv, LANES), lambda s, g, c: (row_map(s, g, c), 0)),
        w_spec(0), w_spec(1), w_spec(2),
        pl.BlockSpec((None, 2 * DN_HEADS, LANES), lambda s, g, c: (l, 0, 0)),
        pl.BlockSpec((None, 2 * DN_HEADS, LANES), lambda s, g, c: (l, 0, 0)),
        pl.BlockSpec((None, 1, DN_DK), lambda s, g, c: (l, 0, 0)),
        pl.BlockSpec((None, hg, DN_DK, DN_DK), lambda s, g, c: (init_map(s), 0, 0, 0)),
        pl.BlockSpec((None, 8, 3 * DN_WIDTH), lambda s, g, c: (init_map(s), 0, 0)),
    ]
    args = [proj, proj, proj, proj, tail, conv_w, conv_w, conv_w, alog_rows, dt_rows, ng, s_init, c_init]
    if mix is None:
        o_spec = pl.BlockSpec((nv, width), lambda s, g, c: (s * nchunk + c, 0))
        o_shape = jax.ShapeDtypeStruct((nseq * nchunk * nv, DN_WIDTH), out_dtype)
        aliases = {}
    else:
        o_spec = pl.BlockSpec((nv, width), lambda s, g, c: (row_map(s, g, c), DIFF_WIDTH // width))
        o_shape = jax.ShapeDtypeStruct(mix.shape, mix.dtype)
        aliases = {len(args): 0}
        in_specs.append(pl.BlockSpec(memory_space=pl.ANY))
        args.append(mix)
    return pl.pallas_call(
        functools.partial(_dn_kernel, nv=nv, hg=hg, into_mix=mix is not None),
        grid=(nseq, 1, nchunk),
        in_specs=in_specs,
        input_output_aliases=aliases,
        out_specs=[
            o_spec,
            pl.BlockSpec((None, hg, DN_DK, DN_DK), lambda s, g, c: (s, 0, 0, 0)),
            pl.BlockSpec((None, 8, 3 * DN_WIDTH), lambda s, g, c: (s, 0, 0)),
        ],
        out_shape=[
            o_shape,
            jax.ShapeDtypeStruct((nseq, DN_HEADS, DN_DK, DN_DK), F32),
            jax.ShapeDtypeStruct((nseq, 8, 3 * DN_WIDTH), F32),
        ],
        scratch_shapes=[
            pltpu.VMEM((3, CHUNK + 8, width), F32),
            pltpu.VMEM((hg, DN_DK, DN_DK), F32),
        ],
        compiler_params=_cparams(("parallel", "arbitrary", "arbitrary")),
        name=f"deltanet_{nv}",
    )(*args)


def kernel(x_prompt, x_sample, cache_k, cache_v, state_ssm, state_conv, page_table, meta_tokens, norm1_g, w_in,
           q_norm_g, k_norm_g, diff_lambda, subln_g, conv_w, a_log, dt_bias, dn_norm_g, w_out, norm2_g, w_up,
           w_down):
    cache_k = _page_view(cache_k)
    cache_v = _page_view(cache_v)

    w_main = w_in.astype(BF16)
    w_tail = jnp.pad(w_in[:, :, MAIN_COLS:], ((0, 0), (0, 0), (0, LANES - 2 * DN_HEADS))).astype(BF16)
    w_out_b = w_out.astype(BF16)
    w_up_b = w_up.astype(BF16)
    w_down_b = w_down.astype(BF16)
    vec = lambda a: a.reshape(DEPTH, 1, a.shape[-1])
    gate_rows = lambda a: jnp.broadcast_to(jnp.pad(a, ((0, 0), (DN_HEADS, 0)))[:, :, None],
                                           (DEPTH, 2 * DN_HEADS, LANES))
    alog_rows = gate_rows(a_log)
    dt_rows = gate_rows(dt_bias)
    norm1_g, norm2_g, q_norm_g, k_norm_g = vec(norm1_g), vec(norm2_g), vec(q_norm_g), vec(k_norm_g)
    subln_g, dn_norm_g = vec(subln_g), vec(dn_norm_g)
    conv_state = jnp.pad(state_conv, ((0, 0), (0, 0), (8 - (CONV_K - 1), 0), (0, 0)))

    x = jnp.concatenate([
        x_prompt.reshape(ROWS_PROMPT, D_MODEL),
        x_sample.reshape(ROWS_SAMPLE, D_MODEL),
        meta_tokens,
        jnp.zeros((N_ROWS - ROW_META0 - N_META, D_MODEL), F32),
    ], axis=0)

    zero_state = jnp.zeros((1, DN_HEADS, DN_DK, DN_DK), F32)
    zero_conv = jnp.zeros((1, 8, 3 * DN_WIDTH), F32)
    outs = [[] for _ in range(8)]
    for l in range(DEPTH):
        lam_init = 0.8 - 0.6 * math.exp(-0.3 * l)
        proj, tail, k2, v2 = _in_proj(x, norm1_g, w_main, w_tail, q_norm_g, k_norm_g, l)
        k_all = proj[:, DIFF_WIDTH:2 * DIFF_WIDTH]
        v_all = proj[:, 2 * DIFF_WIDTH:3 * DIFF_WIDTH]
        meta = slice(ROW_META0, ROW_META0 + N_META)
        kmeta = jnp.pad(k_all[meta], ((0, LANES - N_META), (0, 0)))
        vmeta = jnp.pad(v_all[meta], ((0, LANES - N_META), (0, 0)))

        att_p = _prompt_attn(proj, kmeta, vmeta, diff_lambda, subln_g, l, lam_init)
        att_m = _meta_attn(proj[meta, :DIFF_WIDTH], kmeta, vmeta, diff_lambda, subln_g, l, lam_init)
        att_s = _sample_attn(page_table, proj, cache_k, cache_v, diff_lambda, subln_g, l, lam_init)

        dn_args = (proj, tail, conv_w, alog_rows, dt_rows, dn_norm_g)
        dn_m, s_meta, c_meta = _dn(*dn_args, zero_state, zero_conv, l, nv=N_META, row0=ROW_META0, nseq=1, nchunk=1,
                                   shared_init=True, out_dtype=F32)
        dn_s, s_samp, c_samp = _dn(*dn_args, state_ssm[l], conv_state[l], l, nv=DEC_SEQ, row0=ROW_SAMPLE0,
                                   nseq=DEC_BATCH, nchunk=1, shared_init=False, out_dtype=F32)
        mix, s_prom, c_prom = _dn(*dn_args, s_meta, c_meta, l, nv=CHUNK, row0=0, nseq=BATCH, nchunk=SEQ // CHUNK,
                                  shared_init=True, mix=att_p)
        mix_extra = jnp.concatenate([
            jnp.concatenate([att_s, dn_s], axis=1),
            jnp.concatenate([att_m, dn_m], axis=1),
            jnp.zeros((N_ROWS - ROW_META0 - N_META, MIX_WIDTH), F32),
        ], axis=0).astype(BF16)
        mix = lax.dynamic_update_slice(mix, mix_extra, (ROW_SAMPLE0, 0))
        x = _out_proj(mix, w_out_b, x, l)
        x = _ffn(x, norm2_g, w_up_b, w_down_b, l)

        samp = slice(ROW_SAMPLE0, ROW_SAMPLE0 + ROWS_SAMPLE)
        for t2, prompt_rows, sample_rows in ((k2, outs[0], outs[2]), (v2, outs[1], outs[3])):
            for b in range(BATCH):
                prompt_rows.append(t2[meta.start * NSUB:meta.stop * NSUB])
                prompt_rows.append(t2[b * SEQ * NSUB:(b + 1) * SEQ * NSUB])
            sample_rows.append(t2[samp.start * NSUB:samp.stop * NSUB])
        outs[4].append(s_prom)
        outs[5].append(s_samp)
        outs[6].append(c_prom[:, 8 - (CONV_K - 1):])
        outs[7].append(c_samp[:, 8 - (CONV_K - 1):])

    y_prompt = x[:ROWS_PROMPT].reshape(BATCH, SEQ, D_MODEL)
    y_sample = x[ROW_SAMPLE0:ROW_SAMPLE0 + ROWS_SAMPLE].reshape(DEC_BATCH, DEC_SEQ, D_MODEL)

    def unpaged(rows, lead):
        t = jnp.concatenate(rows, axis=0).reshape(*lead, 2, DIFF_HEADS, HEAD_DIM)
        t = jnp.moveaxis(t, -3, -2)
        return t.reshape(*lead, DIFF_HEADS, DIFF_VDIM)

    kv = [unpaged(outs[i], (DEPTH, BATCH, N_META + SEQ)) for i in (0, 1)]
    kv += [unpaged(outs[i], (DEPTH, DEC_BATCH, DEC_SEQ)) for i in (2, 3)]
    return (y_prompt, y_sample) + tuple(kv) + tuple(jnp.stack(o) for o in outs[4:])
```

```python
import functools
import math

import jax
import jax.numpy as jnp
from jax import lax
from jax.experimental import pallas as pl
from jax.experimental.pallas import tpu as pltpu

F32 = jnp.float32
BF16 = jnp.bfloat16

D_MODEL = 2048
BATCH = 4
SEQ = 2048
DEPTH = 4
DEC_BATCH = 8
DEC_SEQ = 8
PAGE_SIZE = 128
N_META = 16
HEAD_DIM = 128
DIFF_HEADS = 4
DIFF_VDIM = 2 * HEAD_DIM
DIFF_WIDTH = DIFF_HEADS * DIFF_VDIM
DN_HEADS = 8
DN_DK = 128
DN_WIDTH = DN_HEADS * DN_DK
MIX_WIDTH = DIFF_WIDTH + DN_WIDTH
CONV_K = 4
D_FF = 4 * D_MODEL
EPS = 1e-6
MAIN_COLS = 3 * DIFF_WIDTH + 4 * DN_WIDTH

ROWS_PROMPT = BATCH * SEQ
ROWS_SAMPLE = DEC_BATCH * DEC_SEQ
ROW_SAMPLE0 = ROWS_PROMPT
ROW_META0 = ROWS_PROMPT + ROWS_SAMPLE
N_ROWS = 8320
TM = 640
TM_FFN = 832
TM_IN = 1040
LANES = 128
CHUNK = 128
NEG = -1e30
PAGE_ROWS = PAGE_SIZE * 2 * DIFF_HEADS
NSUB = 2 * DIFF_HEADS
ROW_CHUNK = 32
VMEM_LIMIT = 56 * 1024 * 1024


def _cparams(sem):
    return pltpu.CompilerParams(dimension_semantics=sem, vmem_limit_bytes=VMEM_LIMIT)


def _silu(x):
    h = 0.5 * x
    return h + h * jnp.tanh(h)


def _rms(x, gain):
    return x * lax.rsqrt(jnp.mean(x * x, axis=-1, keepdims=True) + EPS) * gain


def _store_paged(dst_ref, cols, tile):
    n_rows = cols.shape[0]
    for j in range(cols.shape[1] // HEAD_DIM):
        piece = tile * (cols.shape[1] // HEAD_DIM) + j
        dst_ref[pl.ds((piece % 2) * DIFF_HEADS + piece // 2, n_rows, stride=NSUB), :] = (
            cols[:, j * HEAD_DIM:(j + 1) * HEAD_DIM])


def _in_proj_kernel(x_ref, g_ref, w_ref, wt_ref, qg_ref, kg_ref, o_ref, ot_ref, k2_ref, v2_ref, xn_ref, *,
                    n_q, n_k):
    n = pl.program_id(1)

    @pl.when(n == 0)
    def _():
        xn_ref[...] = _rms(x_ref[...], g_ref[...]).astype(BF16)
        ot_ref[...] = jnp.dot(xn_ref[...], wt_ref[...], preferred_element_type=F32)

    o_ref[...] = jnp.dot(xn_ref[...], w_ref[...], preferred_element_type=F32)

    def head_norm(gain):
        acc = o_ref[...]
        cols = [_rms(acc[:, j * HEAD_DIM:(j + 1) * HEAD_DIM], gain) for j in range(acc.shape[1] // HEAD_DIM)]
        return jnp.concatenate(cols, axis=1)

    @pl.when(n < n_q)
    def _():
        o_ref[...] = head_norm(qg_ref[...] * (HEAD_DIM ** -0.5))

    for t in range(n_k):
        @pl.when(n == n_q + t)
        def _():
            kn = head_norm(kg_ref[...])
            o_ref[...] = kn
            _store_paged(k2_ref, kn, t)

        @pl.when(n == n_q + n_k + t)
        def _():
            _store_paged(v2_ref, o_ref[...], t)


def _in_proj(x, g, w_main, w_tail, qg, kg, l, tn=1024):
    n_q = DIFF_WIDTH // tn
    return pl.pallas_call(
        functools.partial(_in_proj_kernel, n_q=n_q, n_k=n_q),
        grid=(N_ROWS // TM_IN, MAIN_COLS // tn),
        in_specs=[
            pl.BlockSpec((TM_IN, D_MODEL), lambda m, n: (m, 0)),
            pl.BlockSpec((None, 1, D_MODEL), lambda m, n: (l, 0, 0)),
            pl.BlockSpec((None, D_MODEL, tn), lambda m, n: (l, 0, n)),
            pl.BlockSpec((None, D_MODEL, LANES), lambda m, n: (l, 0, 0)),
            pl.BlockSpec((None, 1, HEAD_DIM), lambda m, n: (l, 0, 0)),
            pl.BlockSpec((None, 1, HEAD_DIM), lambda m, n: (l, 0, 0)),
        ],
        out_specs=[
            pl.BlockSpec((TM_IN, tn), lambda m, n: (m, n)),
            pl.BlockSpec((TM_IN, LANES), lambda m, n: (m, 0)),
            pl.BlockSpec((TM_IN * NSUB, HEAD_DIM), lambda m, n: (m, 0), pipeline_mode=pl.Buffered(1)),
            pl.BlockSpec((TM_IN * NSUB, HEAD_DIM), lambda m, n: (m, 0), pipeline_mode=pl.Buffered(1)),
        ],
        out_shape=[
            jax.ShapeDtypeStruct((N_ROWS, MAIN_COLS), F32),
            jax.ShapeDtypeStruct((N_ROWS, LANES), F32),
            jax.ShapeDtypeStruct((N_ROWS * NSUB, HEAD_DIM), F32),
            jax.ShapeDtypeStruct((N_ROWS * NSUB, HEAD_DIM), F32),
        ],
        scratch_shapes=[pltpu.VMEM((TM_IN, D_MODEL), BF16)],
        compiler_params=_cparams(("parallel", "arbitrary")),
        name="in_proj",
    )(x, g, w_main, w_tail, qg, kg)


def _out_proj_kernel(mix_ref, w_ref, x_ref, o_ref):
    o_ref[...] = x_ref[...] + jnp.dot(mix_ref[...], w_ref[...], preferred_element_type=F32)


def _out_proj(mix, w_out, x, l):
    k = mix.shape[1]
    return pl.pallas_call(
        _out_proj_kernel,
        grid=(N_ROWS // TM,),
        in_specs=[
            pl.BlockSpec((TM, k), lambda m: (m, 0)),
            pl.BlockSpec((None, k, D_MODEL), lambda m: (l, 0, 0)),
            pl.BlockSpec((TM, D_MODEL), lambda m: (m, 0)),
        ],
        out_specs=pl.BlockSpec((TM, D_MODEL), lambda m: (m, 0)),
        out_shape=jax.ShapeDtypeStruct((N_ROWS, D_MODEL), F32),
        compiler_params=_cparams(("parallel",)),
        name="out_proj",
    )(mix, w_out, x)


def _ffn_kernel(x_ref, g_ref, wu_ref, wd_ref, o_ref, xn_ref):
    f = pl.program_id(1)

    @pl.when(f == 0)
    def _():
        xn_ref[...] = _rms(x_ref[...], g_ref[...]).astype(BF16)
        o_ref[...] = x_ref[...]

    h = jnp.maximum(jnp.dot(xn_ref[...], wu_ref[...], preferred_element_type=F32), 0.0)
    o_ref[...] += jnp.dot((h * h).astype(BF16), wd_ref[...], preferred_element_type=F32)


def _ffn(x, g, w_up, w_down, l, tf=1024):
    return pl.pallas_call(
        _ffn_kernel,
        grid=(N_ROWS // TM_FFN, D_FF // tf),
        in_specs=[
            pl.BlockSpec((TM_FFN, D_MODEL), lambda m, f: (m, 0)),
            pl.BlockSpec((None, 1, D_MODEL), lambda m, f: (l, 0, 0)),
            pl.BlockSpec((None, D_MODEL, tf), lambda m, f: (l, 0, f)),
            pl.BlockSpec((None, tf, D_MODEL), lambda m, f: (l, f, 0)),
        ],
        out_specs=pl.BlockSpec((TM_FFN, D_MODEL), lambda m, f: (m, 0)),
        out_shape=jax.ShapeDtypeStruct((N_ROWS, D_MODEL), F32),
        scratch_shapes=[pltpu.VMEM((TM_FFN, D_MODEL), BF16)],
        compiler_params=_cparams(("parallel", "arbitrary")),
        name="ffn",
    )(x, g, w_up, w_down)


def _diff_lambda(lam_ref, lam_init):
    lf = lam_ref[...]
    s1 = jnp.sum(lf[0:1] * lf[1:2], axis=-1, keepdims=True)
    s2 = jnp.sum(lf[2:3] * lf[3:4], axis=-1, keepdims=True)
    return jnp.exp(s1) - jnp.exp(s2) + lam_init


def _diff_finish(o1, o2, lam, g, lam_init):
    return _rms(o1 - lam * o2, g) * (1.0 - lam_init)


def _block_diag_q(q, t):
    z = jnp.zeros((t, HEAD_DIM), q.dtype)
    top = jnp.concatenate([q[:, :HEAD_DIM], z], axis=1)
    bot = jnp.concatenate([z, q[:, HEAD_DIM:]], axis=1)
    return jnp.concatenate([top, bot], axis=0)


_NT = (((1,), (1,)), ((), ()))


def _prompt_attn_kernel(q_ref, k_ref, v_ref, km_ref, vm_ref, lam_ref, g_ref, o_ref,
                        kb_ref, vb_ref, s_ref, p_ref, m_ref, l_ref, a_ref, acc_ref, *, tq, lam_init):
    qi = pl.program_id(2)
    rows = 2 * tq

    @pl.when(qi == 0)
    def _():
        kb_ref[...] = k_ref[...].astype(BF16)
        vb_ref[...] = v_ref[...].astype(BF16)

    q = q_ref[...].astype(BF16)
    q1 = q[:, :HEAD_DIM]
    q2 = q[:, HEAD_DIM:]

    def step(kblk, vblk, nk, visible, first):
        s_ref[0:tq, 0:nk] = lax.dot_general(q1, kblk[:, :HEAD_DIM], _NT, preferred_element_type=F32)
        s_ref[tq:rows, 0:nk] = lax.dot_general(q2, kblk[:, HEAD_DIM:], _NT, preferred_element_type=F32)
        reps = nk // LANES
        for rc in range(rows // ROW_CHUNK):
            rs = slice(rc * ROW_CHUNK, (rc + 1) * ROW_CHUNK)
            s = s_ref[rs, 0:nk]
            if visible is not None:
                r = lax.broadcasted_iota(jnp.int32, (ROW_CHUNK, nk), 0) + (rc * ROW_CHUNK) % tq
                c = lax.broadcasted_iota(jnp.int32, (ROW_CHUNK, nk), 1)
                s = jnp.where(visible(r, c), s, NEG)
            m_new = jnp.broadcast_to(jnp.max(s, axis=-1, keepdims=True), (ROW_CHUNK, LANES))
            if not first:
                m_prev = m_ref[rs, :]
                m_new = jnp.maximum(m_prev, m_new)
                alpha = jnp.exp(m_prev - m_new)
                a_ref[rs, :] = alpha
            m_ref[rs, :] = m_new
            p = jnp.exp(s - jnp.concatenate([m_new] * reps, axis=1))
            p_ref[rs, 0:nk] = p.astype(BF16)
            psum = p[:, 0:LANES]
            for i in range(1, reps):
                psum = psum + p[:, i * LANES:(i + 1) * LANES]
            l_ref[rs, :] = psum if first else alpha * l_ref[rs, :] + psum
        pv = jnp.dot(p_ref[:, 0:nk], vblk, preferred_element_type=F32)
        if first:
            acc_ref[...] = pv
        else:
            alpha = a_ref[...]
            acc_ref[...] = jnp.concatenate([alpha, alpha], axis=1) * acc_ref[...] + pv

    km = km_ref[...].astype(BF16)
    vm = vm_ref[...].astype(BF16)

    @pl.when(qi % 2 == 0)
    def _():
        off = pl.multiple_of(qi * tq, tq)
        step(jnp.concatenate([km, kb_ref[pl.ds(off, tq), :]], axis=0),
             jnp.concatenate([vm, vb_ref[pl.ds(off, tq), :]], axis=0), LANES + tq,
             lambda r, c: jnp.logical_or(c < N_META, jnp.logical_and(c >= LANES, c - LANES <= r)), True)

    @pl.when(qi % 2 == 1)
    def _():
        off = pl.multiple_of((qi - 1) * tq, tq)
        step(jnp.concatenate([km, kb_ref[pl.ds(off, 2 * tq), :]], axis=0),
             jnp.concatenate([vm, vb_ref[pl.ds(off, 2 * tq), :]], axis=0), LANES + 2 * tq,
             lambda r, c: jnp.logical_or(c < N_META, jnp.logical_and(c >= LANES, c - LANES <= r + tq)), True)

    def body(j, carry):
        off = pl.multiple_of(j * (2 * tq), 2 * tq)
        step(kb_ref[pl.ds(off, 2 * tq), :], vb_ref[pl.ds(off, 2 * tq), :], 2 * tq, None, False)
        return carry

    lax.fori_loop(0, qi // 2, body, 0)

    inv_l = 1.0 / jnp.sum(l_ref[...], axis=-1, keepdims=True)
    o = acc_ref[...] * inv_l
    lam = _diff_lambda(lam_ref, lam_init)
    o_ref[...] = _diff_finish(o[:tq], o[tq:], lam, g_ref[...], lam_init).astype(o_ref.dtype)


def _prompt_attn(proj, kmeta, vmeta, lam, subln_g, l, lam_init, tq=512):
    nq = SEQ // tq
    kcol0 = DIFF_WIDTH // DIFF_VDIM
    return pl.pallas_call(
        functools.partial(_prompt_attn_kernel, tq=tq, lam_init=lam_init),
        grid=(BATCH, DIFF_HEADS, nq),
        in_specs=[
            pl.BlockSpec((tq, DIFF_VDIM), lambda b, h, i: (b * nq + i, h)),
            pl.BlockSpec((SEQ, DIFF_VDIM), lambda b, h, i: (b, kcol0 + h)),
            pl.BlockSpec((SEQ, DIFF_VDIM), lambda b, h, i: (b, 2 * kcol0 + h)),
            pl.BlockSpec((LANES, DIFF_VDIM), lambda b, h, i: (0, h)),
            pl.BlockSpec((LANES, DIFF_VDIM), lambda b, h, i: (0, h)),
            pl.BlockSpec((None, 4, HEAD_DIM), lambda b, h, i: (l, 0, 0)),
            pl.BlockSpec((None, 1, DIFF_VDIM), lambda b, h, i: (l, 0, 0)),
        ],
        out_specs=pl.BlockSpec((tq, DIFF_VDIM), lambda b, h, i: (b * nq + i, h)),
        out_shape=jax.ShapeDtypeStruct((N_ROWS, MIX_WIDTH), BF16),
        scratch_shapes=[
            pltpu.VMEM((SEQ, DIFF_VDIM), BF16),
            pltpu.VMEM((SEQ, DIFF_VDIM), BF16),
            pltpu.VMEM((2 * tq, 2 * tq + LANES), F32),
            pltpu.VMEM((2 * tq, 2 * tq + LANES), BF16),
            pltpu.VMEM((2 * tq, LANES), F32),
            pltpu.VMEM((2 * tq, LANES), F32),
            pltpu.VMEM((2 * tq, LANES), F32),
            pltpu.VMEM((2 * tq, DIFF_VDIM), F32),
        ],
        compiler_params=_cparams(("parallel", "parallel", "arbitrary")),
        name="prompt_attn",
    )(proj, proj, proj, kmeta, vmeta, lam, subln_g)


def _meta_attn_kernel(q_ref, km_ref, vm_ref, lam_ref, g_ref, o_ref, *, lam_init):
    t = N_META
    lam = _diff_lambda(lam_ref, lam_init)
    row = lax.broadcasted_iota(jnp.int32, (2 * t, LANES), 0)
    row = jnp.where(row >= t, row - t, row)
    col = lax.broadcasted_iota(jnp.int32, (2 * t, LANES), 1)
    for h in range(DIFF_HEADS):
        sl = slice(h * DIFF_VDIM, (h + 1) * DIFF_VDIM)
        qbd = _block_diag_q(q_ref[:, sl], t).astype(BF16)
        s = lax.dot_general(qbd, km_ref[:, sl].astype(BF16), _NT, preferred_element_type=F32)
        s = jnp.where(col <= row, s, NEG)
        p = jnp.exp(s - jnp.max(s, axis=-1, keepdims=True))
        o = jnp.dot(p.astype(BF16), vm_ref[:, sl].astype(BF16), preferred_element_type=F32)
        o = o / jnp.sum(p, axis=-1, keepdims=True)
        o_ref[:, sl] = _diff_finish(o[:t], o[t:], lam, g_ref[...], lam_init)


def _meta_attn(qmeta, kmeta, vmeta, lam, subln_g, l, lam_init):
    return pl.pallas_call(
        functools.partial(_meta_attn_kernel, lam_init=lam_init),
        grid=(1,),
        in_specs=[
            pl.BlockSpec((N_META, DIFF_WIDTH), lambda i: (0, 0)),
            pl.BlockSpec((LANES, DIFF_WIDTH), lambda i: (0, 0)),
            pl.BlockSpec((LANES, DIFF_WIDTH), lambda i: (0, 0)),
            pl.BlockSpec((None, 4, HEAD_DIM), lambda i: (l, 0, 0)),
            pl.BlockSpec((None, 1, DIFF_VDIM), lambda i: (l, 0, 0)),
        ],
        out_specs=pl.BlockSpec((N_META, DIFF_WIDTH), lambda i: (0, 0)),
        out_shape=jax.ShapeDtypeStruct((N_META, DIFF_WIDTH), F32),
        compiler_params=_cparams(("arbitrary",)),
        name="meta_attn",
    )(qmeta, kmeta, vmeta, lam, subln_g)


def _sample_attn_kernel(pt_ref, q_ref, kn_ref, vn_ref, *rest, pp, lam_init):
    kp_refs = rest[:pp]
    vp_refs = rest[pp:2 * pp]
    lam_ref, g_ref, o_ref, qbd_ref, m_ref, l_ref, acc_ref = rest[2 * pp:]
    t = DEC_SEQ
    rows_h = 2 * t
    g = pl.program_id(1)

    @pl.when(g == 0)
    def _():
        for h in range(DIFF_HEADS):
            qbd_ref[h] = _block_diag_q(q_ref[:, h * DIFF_VDIM:(h + 1) * DIFF_VDIM], t).astype(BF16)
        m_ref[...] = jnp.full_like(m_ref, NEG)
        l_ref[...] = jnp.zeros_like(l_ref)
        acc_ref[...] = jnp.zeros_like(acc_ref)

    def attend(kv_blocks, bias):
        s = [jnp.concatenate([lax.dot_general(qbd_ref[h], k[h], _NT, preferred_element_type=F32)
                              for h in range(DIFF_HEADS)], axis=0) for k, _ in kv_blocks]
        if bias is not None:
            s = [sb + bias for sb in s]
        m_prev = m_ref[...]
        m_tile = s[0]
        for sb in s[1:]:
            m_tile = jnp.maximum(m_tile, sb)
        m_new = jnp.maximum(m_prev, jnp.max(m_tile, axis=-1, keepdims=True))
        alpha = jnp.exp(m_prev - m_new)
        p = [jnp.exp(sb - m_new) for sb in s]
        l_tile = p[0]
        for pb in p[1:]:
            l_tile = l_tile + pb
        l_ref[...] = alpha * l_ref[...] + jnp.sum(l_tile, axis=-1, keepdims=True)
        m_ref[...] = m_new
        for h in range(DIFF_HEADS):
            rs = slice(h * rows_h, (h + 1) * rows_h)
            pv = None
            for pb, (_, v) in zip(p, kv_blocks):
                d = jnp.dot(pb[rs].astype(BF16), v[h], preferred_element_type=F32)
                pv = d if pv is None else pv + d
            acc_ref[rs, :] = alpha[rs] * acc_ref[rs, :] + pv

    def page_heads(ref):
        return [jnp.concatenate([ref[pl.ds(c * DIFF_HEADS + h, PAGE_SIZE, stride=NSUB), :] for c in range(2)],
                                axis=1).astype(BF16) for h in range(DIFF_HEADS)]

    attend([(page_heads(kr), page_heads(vr)) for kr, vr in zip(kp_refs, vp_refs)], None)

    @pl.when(g == pl.num_programs(1) - 1)
    def _():
        zpad = jnp.zeros((LANES - t, DIFF_WIDTH), F32)
        kn = jnp.concatenate([kn_ref[...], zpad], axis=0).astype(BF16)
        vn = jnp.concatenate([vn_ref[...], zpad], axis=0).astype(BF16)
        heads = lambda x: [x[:, h * DIFF_VDIM:(h + 1) * DIFF_VDIM] for h in range(DIFF_HEADS)]
        r = lax.broadcasted_iota(jnp.int32, (DIFF_HEADS * rows_h, LANES), 0)
        c = lax.broadcasted_iota(jnp.int32, (DIFF_HEADS * rows_h, LANES), 1)
        attend([(heads(kn), heads(vn))], jnp.where(c <= r % t, 0.0, NEG))
        lam = _diff_lambda(lam_ref, lam_init)
        o = acc_ref[...] / l_ref[...]
        for h in range(DIFF_HEADS):
            o_ref[:, h * DIFF_VDIM:(h + 1) * DIFF_VDIM] = _diff_finish(
                o[h * rows_h:h * rows_h + t], o[h * rows_h + t:(h + 1) * rows_h], lam, g_ref[...], lam_init)


def _sample_attn(page_table, proj, cache_k, cache_v, lam, subln_g, l, lam_init, pp=16):
    n_pages = page_table.shape[1]
    row0 = ROW_SAMPLE0 // DEC_SEQ
    nq = NSUB * DEC_SEQ
    qkv_spec = lambda c: pl.BlockSpec((DEC_SEQ, DIFF_WIDTH), lambda b, g, pt: (row0 + b, c))
    page_spec = lambda i: pl.BlockSpec((None, None, PAGE_ROWS, HEAD_DIM),
                                       lambda b, g, pt: (l, pt[b, g * pp + i], 0, 0))
    grid_spec = pltpu.PrefetchScalarGridSpec(
        num_scalar_prefetch=1,
        grid=(DEC_BATCH, n_pages // pp),
        in_specs=[qkv_spec(0), qkv_spec(1), qkv_spec(2)]
        + [page_spec(i) for i in range(pp)] + [page_spec(i) for i in range(pp)]
        + [pl.BlockSpec((None, 4, HEAD_DIM), lambda b, g, pt: (l, 0, 0)),
           pl.BlockSpec((None, 1, DIFF_VDIM), lambda b, g, pt: (l, 0, 0))],
        out_specs=pl.BlockSpec((DEC_SEQ, DIFF_WIDTH), lambda b, g, pt: (b, 0)),
        scratch_shapes=[
            pltpu.VMEM((DIFF_HEADS, 2 * DEC_SEQ, DIFF_VDIM), BF16),
            pltpu.VMEM((nq, 1), F32),
            pltpu.VMEM((nq, 1), F32),
            pltpu.VMEM((nq, DIFF_VDIM), F32),
        ],
    )
    return pl.pallas_call(
        functools.partial(_sample_attn_kernel, pp=pp, lam_init=lam_init),
        grid_spec=grid_spec,
        out_shape=jax.ShapeDtypeStruct((ROWS_SAMPLE, DIFF_WIDTH), F32),
        compiler_params=_cparams(("parallel", "arbitrary")),
        name="sample_attn",
    )(page_table, proj, proj, proj, *([cache_k] * pp), *([cache_v] * pp), lam, subln_g)


def _page_view(cache):
    d, n = cache.shape[:2]
    c = cache.reshape(d, n, PAGE_SIZE, DIFF_HEADS, 2, HEAD_DIM)
    return jnp.transpose(c, (0, 1, 2, 4, 3, 5)).reshape(d, n, PAGE_ROWS, HEAD_DIM)


def _dn_kernel(xq_ref, xk_ref, xv_ref, z_ref, ba_ref, wq_ref, wk_ref, wv_ref, alog_ref, dt_ref, ng_ref,
               sinit_ref, cinit_ref, *rest, nv, hg, into_mix):
    o_ref, sout_ref, cout_ref, xp_ref, s_ref = rest[1:] if into_mix else rest
    c = pl.program_id(2)
    last = c == pl.num_programs(2) - 1
    width = hg * DN_DK
    x_refs = (xq_ref, xk_ref, xv_ref)
    w_refs = (wq_ref, wk_ref, wv_ref)
    heads = range(hg)

    @pl.when(c == 0)
    def _():
        s_ref[...] = sinit_ref[...]
        for p in range(3):
            xp_ref[p, 0:8, :] = cinit_ref[:, p * width:(p + 1) * width]

    ys = []
    for p in range(3):
        xp_ref[p, 8:8 + nv, :] = x_refs[p][...]
        if nv < CHUNK:
            xp_ref[p, 8 + nv:, :] = jnp.zeros((CHUNK - nv, width), F32)
        xpv = xp_ref[p]
        y = xpv[8:, :] * w_refs[p][CONV_K - 1:CONV_K, :]
        for d in range(1, CONV_K):
            y = y + pltpu.roll(xpv, d, 0)[8:, :] * w_refs[p][CONV_K - 1 - d:CONV_K - d, :]
        ys.append(_silu(y))

    @pl.when(last)
    def _():
        for p in range(3):
            cout_ref[:, p * width:(p + 1) * width] = xp_ref[p, nv:nv + 8, :]

    if nv == CHUNK:
        for p in range(3):
            xp_ref[p, 0:8, :] = xp_ref[p, CHUNK:CHUNK + 8, :]

    ba = ba_ref[...]
    if nv < CHUNK:
        ba = jnp.concatenate([ba, jnp.zeros((CHUNK - nv, LANES), F32)], axis=0)
    gates = ba.T[0:2 * DN_HEADS, :]
    valid = lax.broadcasted_iota(jnp.int32, gates.shape, 1) < nv
    is_beta = lax.broadcasted_iota(jnp.int32, gates.shape, 0) < DN_HEADS
    ab = gates + dt_ref[...]
    softplus = jnp.maximum(ab, 0.0) + jnp.log1p(jnp.exp(-jnp.abs(ab)))
    g_t = jnp.where(jnp.logical_and(valid, jnp.logical_not(is_beta)), -jnp.exp(alog_ref[...]) * softplus, 0.0)
    ri = lax.broadcasted_iota(jnp.int32, (CHUNK, CHUNK), 0)
    ci = lax.broadcasted_iota(jnp.int32, (CHUNK, CHUNK), 1)
    causal = ri >= ci
    eye = (ri == ci).astype(F32)
    cum_t = jnp.dot(g_t, (ri <= ci).astype(F32), preferred_element_type=F32, precision=lax.Precision.HIGHEST)
    slab = jnp.where(is_beta, jnp.where(valid, jax.nn.sigmoid(gates), 0.0), cum_t)
    cols = jnp.concatenate([slab, jnp.zeros((CHUNK - 2 * DN_HEADS, CHUNK), F32)], axis=0).T

    z = z_ref[...]
    if nv < CHUNK:
        z = jnp.concatenate([z, jnp.zeros((CHUNK - nv, width), F32)], axis=0)

    sl = [slice(j * DN_DK, (j + 1) * DN_DK) for j in heads]
    beta = [cols[:, j:j + 1] for j in heads]
    gc = [cols[:, DN_HEADS + j:DN_HEADS + j + 1] for j in heads]
    gr = [cum_t[DN_HEADS + j:DN_HEADS + j + 1, :] for j in heads]
    g_last = [gc[j][CHUNK - 1:CHUNK, :] for j in heads]

    def l2n(x, scale):
        return x * (lax.rsqrt(jnp.sum(x * x, axis=-1, keepdims=True) + EPS) * scale)

    q = [l2n(ys[0][:, sl[j]], DN_DK ** -0.5) for j in heads]
    k = [l2n(ys[1][:, sl[j]], 1.0) for j in heads]
    v = [ys[2][:, sl[j]] for j in heads]
    k_t = [k[j].T for j in heads]
    s_old = [s_ref[j] for j in heads]
    qk16 = [jnp.concatenate([q[j], k[j]], axis=0).astype(BF16) for j in heads]
    a1 = [jnp.dot(qk16[j], k_t[j].astype(BF16), preferred_element_type=F32) for j in heads]
    a2 = [jnp.dot(qk16[j], s_old[j].astype(BF16), preferred_element_type=F32) for j in heads]
    decay = [jnp.exp(jnp.where(causal, gc[j] - gr[j], NEG)) for j in heads]
    e_g = [jnp.exp(gc[j]) for j in heads]
    rhs = [(beta[j] * (v[j] - e_g[j] * a2[j][CHUNK:])).astype(BF16) for j in heads]
    xpow = [a1[j][CHUNK:] * ((decay[j] - eye) * (-beta[j])) for j in heads]
    t_inv = [eye + xpow[j] for j in heads]
    xb = [xpow[j].astype(BF16) for j in heads]
    for _ in range(int(math.log2(CHUNK)) - 1):
        xb = [jnp.dot(xb[j], xb[j], preferred_element_type=F32).astype(BF16) for j in heads]
        t_inv = [t_inv[j] + jnp.dot(t_inv[j].astype(BF16), xb[j], preferred_element_type=F32) for j in heads]
    u = [jnp.dot(t_inv[j].astype(BF16), rhs[j], preferred_element_type=F32).astype(BF16) for j in heads]
    o = [e_g[j] * a2[j][:CHUNK] + jnp.dot((a1[j][:CHUNK] * decay[j]).astype(BF16), u[j],
                                          preferred_element_type=F32) for j in heads]
    for j in heads:
        tail = jnp.exp(g_last[j] - gr[j])
        s_ref[j] = jnp.exp(g_last[j]) * s_old[j] + jnp.dot((k_t[j] * tail).astype(BF16), u[j],
                                                           preferred_element_type=F32)
    for j in heads:
        zj = z[:, sl[j]]
        on = _rms(o[j], ng_ref[...]) * _silu(zj)
        o_ref[:, sl[j]] = on[:nv].astype(o_ref.dtype)

    @pl.when(last)
    def _():
        sout_ref[...] = s_ref[...]


def _dn(proj, tail, conv_w, alog_rows, dt_rows, ng, s_init, c_init, l, *, nv, row0, nseq, nchunk, shared_init,
        out_dtype=None, mix=None):
    hg = DN_HEADS
    width = hg * DN_DK
    q0 = 3 * DIFF_WIDTH // width
    rb0 = row0 // nv
    row_map = lambda s, g, c: rb0 + s * nchunk + c
    init_map = (lambda s: 0) if shared_init else (lambda s: s)
    x_spec = lambda sec: pl.BlockSpec((nv, width), lambda s, g, c: (row_map(s, g, c), q0 + sec))
    w_spec = lambda sec: pl.BlockSpec((None, CONV_K, width), lambda s, g, c: (l, 0, sec))
    in_specs = [
        x_spec(0), x_spec(1), x_spec(2), x_spec(3),
        pl.BlockSpec((nv, LANES), lambda s, g, c: (row_map(s, g, c), 0)),
        w_spec(0), w_spec(1), w_spec(2),
        pl.BlockSpec((None, 2 * DN_HEADS, LANES), lambda s, g, c: (l, 0, 0)),
        pl.BlockSpec((None, 2 * DN_HEADS, LANES), lambda s, g, c: (l, 0, 0)),
        pl.BlockSpec((None, 1, DN_DK), lambda s, g, c: (l, 0, 0)),
        pl.BlockSpec((None, hg, DN_DK, DN_DK), lambda s, g, c: (init_map(s), 0, 0, 0)),
        pl.BlockSpec((None, 8, 3 * DN_WIDTH), lambda s, g, c: (init_map(s), 0, 0)),
    ]
    args = [proj, proj, proj, proj, tail, conv_w, conv_w, conv_w, alog_rows, dt_rows, ng, s_init, c_init]
    if mix is None:
        o_spec = pl.BlockSpec((nv, width), lambda s, g, c: (s * nchunk + c, 0))
        o_shape = jax.ShapeDtypeStruct((nseq * nchunk * nv, DN_WIDTH), out_dtype)
        aliases = {}
    else:
        o_spec = pl.BlockSpec((nv, width), lambda s, g, c: (row_map(s, g, c), DIFF_WIDTH // width))
        o_shape = jax.ShapeDtypeStruct(mix.shape, mix.dtype)
        aliases = {len(args): 0}
        in_specs.append(pl.BlockSpec(memory_space=pl.ANY))
        args.append(mix)
    return pl.pallas_call(
        functools.partial(_dn_kernel, nv=nv, hg=hg, into_mix=mix is not None),
        grid=(nseq, 1, nchunk),
        in_specs=in_specs,
        input_output_aliases=aliases,
        out_specs=[
            o_spec,
            pl.BlockSpec((None, hg, DN_DK, DN_DK), lambda s, g, c: (s, 0, 0, 0)),
            pl.BlockSpec((None, 8, 3 * DN_WIDTH), lambda s, g, c: (s, 0, 0)),
        ],
        out_shape=[
            o_shape,
            jax.ShapeDtypeStruct((nseq, DN_HEADS, DN_DK, DN_DK), F32),
            jax.ShapeDtypeStruct((nseq, 8, 3 * DN_WIDTH), F32),
        ],
        scratch_shapes=[
            pltpu.VMEM((3, CHUNK + 8, width), F32),
            pltpu.VMEM((hg, DN_DK, DN_DK), F32),
        ],
        compiler_params=_cparams(("parallel", "arbitrary", "arbitrary")),
        name=f"deltanet_{nv}",
    )(*args)


def kernel(x_prompt, x_sample, cache_k, cache_v, state_ssm, state_conv, page_table, meta_tokens, norm1_g, w_in,
           q_norm_g, k_norm_g, diff_lambda, subln_g, conv_w, a_log, dt_bias, dn_norm_g, w_out, norm2_g, w_up,
           w_down):
    cache_k = _page_view(cache_k)
    cache_v = _page_view(cache_v)

    w_main = w_in.astype(BF16)
    w_tail = jnp.pad(w_in[:, :, MAIN_COLS:], ((0, 0), (0, 0), (0, LANES - 2 * DN_HEADS))).astype(BF16)
    w_out_b = w_out.astype(BF16)
    w_up_b = w_up.astype(BF16)
    w_down_b = w_down.astype(BF16)
    vec = lambda a: a.reshape(DEPTH, 1, a.shape[-1])
    gate_rows = lambda a: jnp.broadcast_to(jnp.pad(a, ((0, 0), (DN_HEADS, 0)))[:, :, None],
                                           (DEPTH, 2 * DN_HEADS, LANES))
    alog_rows = gate_rows(a_log)
    dt_rows = gate_rows(dt_bias)
    norm1_g, norm2_g, q_norm_g, k_norm_g = vec(norm1_g), vec(norm2_g), vec(q_norm_g), vec(k_norm_g)
    subln_g, dn_norm_g = vec(subln_g), vec(dn_norm_g)
    conv_state = jnp.pad(state_conv, ((0, 0), (0, 0), (8 - (CONV_K - 1), 0), (0, 0)))

    x = jnp.concatenate([
        x_prompt.reshape(ROWS_PROMPT, D_MODEL),
        x_sample.reshape(ROWS_SAMPLE, D_MODEL),
        meta_tokens,
        jnp.zeros((N_ROWS - ROW_META0 - N_META, D_MODEL), F32),
    ], axis=0)

    zero_state = jnp.zeros((1, DN_HEADS, DN_DK, DN_DK), F32)
    zero_conv = jnp.zeros((1, 8, 3 * DN_WIDTH), F32)
    outs = [[] for _ in range(8)]
    for l in range(DEPTH):
        lam_init = 0.8 - 0.6 * math.exp(-0.3 * l)
        proj, tail, k2, v2 = _in_proj(x, norm1_g, w_main, w_tail, q_norm_g, k_norm_g, l)
        k_all = proj[:, DIFF_WIDTH:2 * DIFF_WIDTH]
        v_all = proj[:, 2 * DIFF_WIDTH:3 * DIFF_WIDTH]
        meta = slice(ROW_META0, ROW_META0 + N_META)
        kmeta = jnp.pad(k_all[meta], ((0, LANES - N_META), (0, 0)))
        vmeta = jnp.pad(v_all[meta], ((0, LANES - N_META), (0, 0)))

        att_p = _prompt_attn(proj, kmeta, vmeta, diff_lambda, subln_g, l, lam_init)
        att_m = _meta_attn(proj[meta, :DIFF_WIDTH], kmeta, vmeta, diff_lambda, subln_g, l, lam_init)
        att_s = _sample_attn(page_table, proj, cache_k, cache_v, diff_lambda, subln_g, l, lam_init)

        dn_args = (proj, tail, conv_w, alog_rows, dt_rows, dn_norm_g)
        dn_m, s_meta, c_meta = _dn(*dn_args, zero_state, zero_conv, l, nv=N_META, row0=ROW_META0, nseq=1, nchunk=1,
                                   shared_init=True, out_dtype=F32)
        dn_s, s_samp, c_samp = _dn(*dn_args, state_ssm[l], conv_state[l], l, nv=DEC_SEQ, row0=ROW_SAMPLE0,
                                   nseq=DEC_BATCH, nchunk=1, shared_init=False, out_dtype=F32)
        mix, s_prom, c_prom = _dn(*dn_args, s_meta, c_meta, l, nv=CHUNK, row0=0, nseq=BATCH, nchunk=SEQ // CHUNK,
                                  shared_init=True, mix=att_p)
        mix_extra = jnp.concatenate([
            jnp.concatenate([att_s, dn_s], axis=1),
            jnp.concatenate([att_m, dn_m], axis=1),
            jnp.zeros((N_ROWS - ROW_META0 - N_META, MIX_WIDTH), F32),
        ], axis=0).astype(BF16)
        mix = lax.dynamic_update_slice(mix, mix_extra, (ROW_SAMPLE0, 0))
        x = _out_proj(mix, w_out_b, x, l)
        x = _ffn(x, norm2_g, w_up_b, w_down_b, l)

        samp = slice(ROW_SAMPLE0, ROW_SAMPLE0 + ROWS_SAMPLE)
        for t2, prompt_rows, sample_rows in ((k2, outs[0], outs[2]), (v2, outs[1], outs[3])):
            for b in range(BATCH):
                prompt_rows.append(t2[meta.start * NSUB:meta.stop * NSUB])
                prompt_rows.append(t2[b * SEQ * NSUB:(b + 1) * SEQ * NSUB])
            sample_rows.append(t2[samp.start * NSUB:samp.stop * NSUB])
        outs[4].append(s_prom)
        outs[5].append(s_samp)
        outs[6].append(c_prom[:, 8 - (CONV_K - 1):])
        outs[7].append(c_samp[:, 8 - (CONV_K - 1):])

    y_prompt = x[:ROWS_PROMPT].reshape(BATCH, SEQ, D_MODEL)
    y_sample = x[ROW_SAMPLE0:ROW_SAMPLE0 + ROWS_SAMPLE].reshape(DEC_BATCH, DEC_SEQ, D_MODEL)

    def unpaged(rows, lead):
        t = jnp.concatenate(rows, axis=0).reshape(*lead, 2, DIFF_HEADS, HEAD_DIM)
        t = jnp.moveaxis(t, -3, -2)
        return t.reshape(*lead, DIFF_HEADS, DIFF_VDIM)

    kv = [unpaged(outs[i], (DEPTH, BATCH, N_META + SEQ)) for i in (0, 1)]
    kv += [unpaged(outs[i], (DEPTH, DEC_BATCH, DEC_SEQ)) for i in (2, 3)]
    return (y_prompt, y_sample) + tuple(kv) + tuple(jnp.stack(o) for o in outs[4:])
```

```python
import functools
import math

import jax
import jax.numpy as jnp
from jax import lax
from jax.experimental import pallas as pl
from jax.experimental.pallas import tpu as pltpu

F32 = jnp.float32
BF16 = jnp.bfloat16

D_MODEL = 2048
BATCH = 4
SEQ = 2048
DEPTH = 4
DEC_BATCH = 8
DEC_SEQ = 8
PAGE_SIZE = 128
N_META = 16
HEAD_DIM = 128
DIFF_HEADS = 4
DIFF_VDIM = 2 * HEAD_DIM
DIFF_WIDTH = DIFF_HEADS * DIFF_VDIM
DN_HEADS = 8
DN_DK = 128
DN_WIDTH = DN_HEADS * DN_DK
MIX_WIDTH = DIFF_WIDTH + DN_WIDTH
CONV_K = 4
D_FF = 4 * D_MODEL
EPS = 1e-6
MAIN_COLS = 3 * DIFF_WIDTH + 4 * DN_WIDTH

ROWS_PROMPT = BATCH * SEQ
ROWS_SAMPLE = DEC_BATCH * DEC_SEQ
ROW_SAMPLE0 = ROWS_PROMPT
ROW_META0 = ROWS_PROMPT + ROWS_SAMPLE
N_ROWS = 8320
TM = 640
TM_FFN = 832
TM_IN = 1040
LANES = 128
CHUNK = 128
NEG = -1e30
PAGE_ROWS = PAGE_SIZE * 2 * DIFF_HEADS
NSUB = 2 * DIFF_HEADS
ROW_CHUNK = 32
VMEM_LIMIT = 56 * 1024 * 1024


def _cparams(sem):
    return pltpu.CompilerParams(dimension_semantics=sem, vmem_limit_bytes=VMEM_LIMIT)


def _silu(x):
    h = 0.5 * x
    return h + h * jnp.tanh(h)


def _rms(x, gain):
    return x * lax.rsqrt(jnp.mean(x * x, axis=-1, keepdims=True) + EPS) * gain


def _store_paged(dst_ref, cols, tile):
    n_rows = cols.shape[0]
    for j in range(cols.shape[1] // HEAD_DIM):
        piece = tile * (cols.shape[1] // HEAD_DIM) + j
        dst_ref[pl.ds((piece % 2) * DIFF_HEADS + piece // 2, n_rows, stride=NSUB), :] = (
            cols[:, j * HEAD_DIM:(j + 1) * HEAD_DIM])


def _in_proj_kernel(x_ref, g_ref, w_ref, wt_ref, qg_ref, kg_ref, o_ref, ot_ref, k2_ref, v2_ref, xn_ref, *,
                    n_q, n_k):
    n = pl.program_id(1)

    @pl.when(n == 0)
    def _():
        xn_ref[...] = _rms(x_ref[...], g_ref[...]).astype(BF16)
        ot_ref[...] = jnp.dot(xn_ref[...], wt_ref[...], preferred_element_type=F32)

    o_ref[...] = jnp.dot(xn_ref[...], w_ref[...], preferred_element_type=F32)

    def head_norm(gain):
        acc = o_ref[...]
        cols = [_rms(acc[:, j * HEAD_DIM:(j + 1) * HEAD_DIM], gain) for j in range(acc.shape[1] // HEAD_DIM)]
        return jnp.concatenate(cols, axis=1)

    @pl.when(n < n_q)
    def _():
        o_ref[...] = head_norm(qg_ref[...] * (HEAD_DIM ** -0.5))

    for t in range(n_k):
        @pl.when(n == n_q + t)
        def _():
            kn = head_norm(kg_ref[...])
            o_ref[...] = kn
            _store_paged(k2_ref, kn, t)

        @pl.when(n == n_q + n_k + t)
        def _():
            _store_paged(v2_ref, o_ref[...], t)


def _in_proj(x, g, w_main, w_tail, qg, kg, l, tn=1024):
    n_q = DIFF_WIDTH // tn
    return pl.pallas_call(
        functools.partial(_in_proj_kernel, n_q=n_q, n_k=n_q),
        grid=(N_ROWS // TM_IN, MAIN_COLS // tn),
        in_specs=[
            pl.BlockSpec((TM_IN, D_MODEL), lambda m, n: (m, 0)),
            pl.BlockSpec((None, 1, D_MODEL), lambda m, n: (l, 0, 0)),
            pl.BlockSpec((None, D_MODEL, tn), lambda m, n: (l, 0, n)),
            pl.BlockSpec((None, D_MODEL, LANES), lambda m, n: (l, 0, 0)),
            pl.BlockSpec((None, 1, HEAD_DIM), lambda m, n: (l, 0, 0)),
            pl.BlockSpec((None, 1, HEAD_DIM), lambda m, n: (l, 0, 0)),
        ],
        out_specs=[
            pl.BlockSpec((TM_IN, tn), lambda m, n: (m, n)),
            pl.BlockSpec((TM_IN, LANES), lambda m, n: (m, 0)),
            pl.BlockSpec((TM_IN * NSUB, HEAD_DIM), lambda m, n: (m, 0), pipeline_mode=pl.Buffered(1)),
            pl.BlockSpec((TM_IN * NSUB, HEAD_DIM), lambda m, n: (m, 0), pipeline_mode=pl.Buffered(1)),
        ],
        out_shape=[
            jax.ShapeDtypeStruct((N_ROWS, MAIN_COLS), F32),
            jax.ShapeDtypeStruct((N_ROWS, LANES), F32),
            jax.ShapeDtypeStruct((N_ROWS * NSUB, HEAD_DIM), F32),
            jax.ShapeDtypeStruct((N_ROWS * NSUB, HEAD_DIM), F32),
        ],
        scratch_shapes=[pltpu.VMEM((TM_IN, D_MODEL), BF16)],
        compiler_params=_cparams(("parallel", "arbitrary")),
        name="in_proj",
    )(x, g, w_main, w_tail, qg, kg)


def _out_proj_kernel(mix_ref, w_ref, x_ref, o_ref):
    o_ref[...] = x_ref[...] + jnp.dot(mix_ref[...], w_ref[...], preferred_element_type=F32)


def _out_proj(mix, w_out, x, l):
    k = mix.shape[1]
    return pl.pallas_call(
        _out_proj_kernel,
        grid=(N_ROWS // TM,),
        in_specs=[
            pl.BlockSpec((TM, k), lambda m: (m, 0)),
            pl.BlockSpec((None, k, D_MODEL), lambda m: (l, 0, 0)),
            pl.BlockSpec((TM, D_MODEL), lambda m: (m, 0)),
        ],
        out_specs=pl.BlockSpec((TM, D_MODEL), lambda m: (m, 0)),
        out_shape=jax.ShapeDtypeStruct((N_ROWS, D_MODEL), F32),
        compiler_params=_cparams(("parallel",)),
        name="out_proj",
    )(mix, w_out, x)


def _ffn_kernel(x_ref, g_ref, wu_ref, wd_ref, o_ref, xn_ref):
    f = pl.program_id(1)

    @pl.when(f == 0)
    def _():
        xn_ref[...] = _rms(x_ref[...], g_ref[...]).astype(BF16)
        o_ref[...] = x_ref[...]

    h = jnp.maximum(jnp.dot(xn_ref[...], wu_ref[...], preferred_element_type=F32), 0.0)
    o_ref[...] += jnp.dot((h * h).astype(BF16), wd_ref[...], preferred_element_type=F32)


def _ffn(x, g, w_up, w_down, l, tf=1024):
    return pl.pallas_call(
        _ffn_kernel,
        grid=(N_ROWS // TM_FFN, D_FF // tf),
        in_specs=[
            pl.BlockSpec((TM_FFN, D_MODEL), lambda m, f: (m, 0)),
            pl.BlockSpec((None, 1, D_MODEL), lambda m, f: (l, 0, 0)),
            pl.BlockSpec((None, D_MODEL, tf), lambda m, f: (l, 0, f)),
            pl.BlockSpec((None, tf, D_MODEL), lambda m, f: (l, f, 0)),
        ],
        out_specs=pl.BlockSpec((TM_FFN, D_MODEL), lambda m, f: (m, 0)),
        out_shape=jax.ShapeDtypeStruct((N_ROWS, D_MODEL), F32),
        scratch_shapes=[pltpu.VMEM((TM_FFN, D_MODEL), BF16)],
        compiler_params=_cparams(("parallel", "arbitrary")),
        name="ffn",
    )(x, g, w_up, w_down)


def _diff_lambda(lam_ref, lam_init):
    lf = lam_ref[...]
    s1 = jnp.sum(lf[0:1] * lf[1:2], axis=-1, keepdims=True)
    s2 = jnp.sum(lf[2:3] * lf[3:4], axis=-1, keepdims=True)
    return jnp.exp(s1) - jnp.exp(s2) + lam_init


def _diff_finish(o1, o2, lam, g, lam_init):
    return _rms(o1 - lam * o2, g) * (1.0 - lam_init)


def _block_diag_q(q, t):
    z = jnp.zeros((t, HEAD_DIM), q.dtype)
    top = jnp.concatenate([q[:, :HEAD_DIM], z], axis=1)
    bot = jnp.concatenate([z, q[:, HEAD_DIM:]], axis=1)
    return jnp.concatenate([top, bot], axis=0)


_NT = (((1,), (1,)), ((), ()))


def _prompt_attn_kernel(q_ref, k_ref, v_ref, km_ref, vm_ref, lam_ref, g_ref, o_ref,
                        kb_ref, vb_ref, s_ref, p_ref, m_ref, l_ref, a_ref, acc_ref, *, tq, lam_init):
    qi = pl.program_id(2)
    rows = 2 * tq

    @pl.when(qi == 0)
    def _():
        kb_ref[...] = k_ref[...].astype(BF16)
        vb_ref[...] = v_ref[...].astype(BF16)

    q = q_ref[...].astype(BF16)
    q1 = q[:, :HEAD_DIM]
    q2 = q[:, HEAD_DIM:]

    def step(kblk, vblk, nk, visible, first):
        s_ref[0:tq, 0:nk] = lax.dot_general(q1, kblk[:, :HEAD_DIM], _NT, preferred_element_type=F32)
        s_ref[tq:rows, 0:nk] = lax.dot_general(q2, kblk[:, HEAD_DIM:], _NT, preferred_element_type=F32)
        reps = nk // LANES
        for rc in range(rows // ROW_CHUNK):
            rs = slice(rc * ROW_CHUNK, (rc + 1) * ROW_CHUNK)
            s = s_ref[rs, 0:nk]
            if visible is not None:
                r = lax.broadcasted_iota(jnp.int32, (ROW_CHUNK, nk), 0) + (rc * ROW_CHUNK) % tq
                c = lax.broadcasted_iota(jnp.int32, (ROW_CHUNK, nk), 1)
                s = jnp.where(visible(r, c), s, NEG)
            m_new = jnp.broadcast_to(jnp.max(s, axis=-1, keepdims=True), (ROW_CHUNK, LANES))
            if not first:
                m_prev = m_ref[rs, :]
                m_new = jnp.maximum(m_prev, m_new)
                alpha = jnp.exp(m_prev - m_new)
                a_ref[rs, :] = alpha
            m_ref[rs, :] = m_new
            p = jnp.exp(s - jnp.concatenate([m_new] * reps, axis=1))
            p_ref[rs, 0:nk] = p.astype(BF16)
            psum = p[:, 0:LANES]
            for i in range(1, reps):
                psum = psum + p[:, i * LANES:(i + 1) * LANES]
            l_ref[rs, :] = psum if first else alpha * l_ref[rs, :] + psum
        pv = jnp.dot(p_ref[:, 0:nk], vblk, preferred_element_type=F32)
        if first:
            acc_ref[...] = pv
        else:
            alpha = a_ref[...]
            acc_ref[...] = jnp.concatenate([alpha, alpha], axis=1) * acc_ref[...] + pv

    km = km_ref[...].astype(BF16)
    vm = vm_ref[...].astype(BF16)

    @pl.when(qi % 2 == 0)
    def _():
        off = pl.multiple_of(qi * tq, tq)
        step(jnp.concatenate([km, kb_ref[pl.ds(off, tq), :]], axis=0),
             jnp.concatenate([vm, vb_ref[pl.ds(off, tq), :]], axis=0), LANES + tq,
             lambda r, c: jnp.logical_or(c < N_META, jnp.logical_and(c >= LANES, c - LANES <= r)), True)

    @pl.when(qi % 2 == 1)
    def _():
        off = pl.multiple_of((qi - 1) * tq, tq)
        step(jnp.concatenate([km, kb_ref[pl.ds(off, 2 * tq), :]], axis=0),
             jnp.concatenate([vm, vb_ref[pl.ds(off, 2 * tq), :]], axis=0), LANES + 2 * tq,
             lambda r, c: jnp.logical_or(c < N_META, jnp.logical_and(c >= LANES, c - LANES <= r + tq)), True)

    def body(j, carry):
        off = pl.multiple_of(j * (2 * tq), 2 * tq)
        step(kb_ref[pl.ds(off, 2 * tq), :], vb_ref[pl.ds(off, 2 * tq), :], 2 * tq, None, False)
        return carry

    lax.fori_loop(0, qi // 2, body, 0)

    inv_l = 1.0 / jnp.sum(l_ref[...], axis=-1, keepdims=True)
    o = acc_ref[...] * inv_l
    lam = _diff_lambda(lam_ref, lam_init)
    o_ref[...] = _diff_finish(o[:tq], o[tq:], lam, g_ref[...], lam_init).astype(o_ref.dtype)


def _prompt_attn(proj, kmeta, vmeta, lam, subln_g, l, lam_init, tq=512):
    nq = SEQ // tq
    kcol0 = DIFF_WIDTH // DIFF_VDIM
    return pl.pallas_call(
        functools.partial(_prompt_attn_kernel, tq=tq, lam_init=lam_init),
        grid=(BATCH, DIFF_HEADS, nq),
        in_specs=[
            pl.BlockSpec((tq, DIFF_VDIM), lambda b, h, i: (b * nq + i, h)),
            pl.BlockSpec((SEQ, DIFF_VDIM), lambda b, h, i: (b, kcol0 + h)),
            pl.BlockSpec((SEQ, DIFF_VDIM), lambda b, h, i: (b, 2 * kcol0 + h)),
            pl.BlockSpec((LANES, DIFF_VDIM), lambda b, h, i: (0, h)),
            pl.BlockSpec((LANES, DIFF_VDIM), lambda b, h, i: (0, h)),
            pl.BlockSpec((None, 4, HEAD_DIM), lambda b, h, i: (l, 0, 0)),
            pl.BlockSpec((None, 1, DIFF_VDIM), lambda b, h, i: (l, 0, 0)),
        ],
        out_specs=pl.BlockSpec((tq, DIFF_VDIM), lambda b, h, i: (b * nq + i, h)),
        out_shape=jax.ShapeDtypeStruct((N_ROWS, MIX_WIDTH), BF16),
        scratch_shapes=[
            pltpu.VMEM((SEQ, DIFF_VDIM), BF16),
            pltpu.VMEM((SEQ, DIFF_VDIM), BF16),
            pltpu.VMEM((2 * tq, 2 * tq + LANES), F32),
            pltpu.VMEM((2 * tq, 2 * tq + LANES), BF16),
            pltpu.VMEM((2 * tq, LANES), F32),
            pltpu.VMEM((2 * tq, LANES), F32),
            pltpu.VMEM((2 * tq, LANES), F32),
            pltpu.VMEM((2 * tq, DIFF_VDIM), F32),
        ],
        compiler_params=_cparams(("parallel", "parallel", "arbitrary")),
        name="prompt_attn",
    )(proj, proj, proj, kmeta, vmeta, lam, subln_g)


def _meta_attn_kernel(q_ref, km_ref, vm_ref, lam_ref, g_ref, o_ref, *, lam_init):
    t = N_META
    lam = _diff_lambda(lam_ref, lam_init)
    row = lax.broadcasted_iota(jnp.int32, (2 * t, LANES), 0)
    row = jnp.where(row >= t, row - t, row)
    col = lax.broadcasted_iota(jnp.int32, (2 * t, LANES), 1)
    for h in range(DIFF_HEADS):
        sl = slice(h * DIFF_VDIM, (h + 1) * DIFF_VDIM)
        qbd = _block_diag_q(q_ref[:, sl], t).astype(BF16)
        s = lax.dot_general(qbd, km_ref[:, sl].astype(BF16), _NT, preferred_element_type=F32)
        s = jnp.where(col <= row, s, NEG)
        p = jnp.exp(s - jnp.max(s, axis=-1, keepdims=True))
        o = jnp.dot(p.astype(BF16), vm_ref[:, sl].astype(BF16), preferred_element_type=F32)
        o = o / jnp.sum(p, axis=-1, keepdims=True)
        o_ref[:, sl] = _diff_finish(o[:t], o[t:], lam, g_ref[...], lam_init)


def _meta_attn(qmeta, kmeta, vmeta, lam, subln_g, l, lam_init):
    return pl.pallas_call(
        functools.partial(_meta_attn_kernel, lam_init=lam_init),
        grid=(1,),
        in_specs=[
            pl.BlockSpec((N_META, DIFF_WIDTH), lambda i: (0, 0)),
            pl.BlockSpec((LANES, DIFF_WIDTH), lambda i: (0, 0)),
            pl.BlockSpec((LANES, DIFF_WIDTH), lambda i: (0, 0)),
            pl.BlockSpec((None, 4, HEAD_DIM), lambda i: (l, 0, 0)),
            pl.BlockSpec((None, 1, DIFF_VDIM), lambda i: (l, 0, 0)),
        ],
        out_specs=pl.BlockSpec((N_META, DIFF_WIDTH), lambda i: (0, 0)),
        out_shape=jax.ShapeDtypeStruct((N_META, DIFF_WIDTH), F32),
        compiler_params=_cparams(("arbitrary",)),
        name="meta_attn",
    )(qmeta, kmeta, vmeta, lam, subln_g)


def _sample_attn_kernel(pt_ref, q_ref, kn_ref, vn_ref, *rest, pp, lam_init):
    kp_refs = rest[:pp]
    vp_refs = rest[pp:2 * pp]
    lam_ref, g_ref, o_ref, qbd_ref, m_ref, l_ref, acc_ref = rest[2 * pp:]
    t = DEC_SEQ
    rows_h = 2 * t
    g = pl.program_id(1)

    @pl.when(g == 0)
    def _():
        for h in range(DIFF_HEADS):
            qbd_ref[h] = _block_diag_q(q_ref[:, h * DIFF_VDIM:(h + 1) * DIFF_VDIM], t).astype(BF16)
        m_ref[...] = jnp.full_like(m_ref, NEG)
        l_ref[...] = jnp.zeros_like(l_ref)
        acc_ref[...] = jnp.zeros_like(acc_ref)

    def attend(kv_blocks, bias):
        s = [jnp.concatenate([lax.dot_general(qbd_ref[h], k[h], _NT, preferred_element_type=F32)
                              for h in range(DIFF_HEADS)], axis=0) for k, _ in kv_blocks]
        if bias is not None:
            s = [sb + bias for sb in s]
        m_prev = m_ref[...]
        m_tile = s[0]
        for sb in s[1:]:
            m_tile = jnp.maximum(m_tile, sb)
        m_new = jnp.maximum(m_prev, jnp.max(m_tile, axis=-1, keepdims=True))
        alpha = jnp.exp(m_prev - m_new)
        p = [jnp.exp(sb - m_new) for sb in s]
        l_tile = p[0]
        for pb in p[1:]:
            l_tile = l_tile + pb
        l_ref[...] = alpha * l_ref[...] + jnp.sum(l_tile, axis=-1, keepdims=True)
        m_ref[...] = m_new
        for h in range(DIFF_HEADS):
            rs = slice(h * rows_h, (h + 1) * rows_h)
            pv = None
            for pb, (_, v) in zip(p, kv_blocks):
                d = jnp.dot(pb[rs].astype(BF16), v[h], preferred_element_type=F32)
                pv = d if pv is None else pv + d
            acc_ref[rs, :] = alpha[rs] * acc_ref[rs, :] + pv

    def page_heads(ref):
        return [jnp.concatenate([ref[pl.ds(c * DIFF_HEADS + h, PAGE_SIZE, stride=NSUB), :] for c in range(2)],
                                axis=1).astype(BF16) for h in range(DIFF_HEADS)]

    attend([(page_heads(kr), page_heads(vr)) for kr, vr in zip(kp_refs, vp_refs)], None)

    @pl.when(g == pl.num_programs(1) - 1)
    def _():
        zpad = jnp.zeros((LANES - t, DIFF_WIDTH), F32)
        kn = jnp.concatenate([kn_ref[...], zpad], axis=0).astype(BF16)
        vn = jnp.concatenate([vn_ref[...], zpad], axis=0).astype(BF16)
        heads = lambda x: [x[:, h * DIFF_VDIM:(h + 1) * DIFF_VDIM] for h in range(DIFF_HEADS)]
        r = lax.broadcasted_iota(jnp.int32, (DIFF_HEADS * rows_h, LANES), 0)
        c = lax.broadcasted_iota(jnp.int32, (DIFF_HEADS * rows_h, LANES), 1)
        attend([(heads(kn), heads(vn))], jnp.where(c <= r % t, 0.0, NEG))
        lam = _diff_lambda(lam_ref, lam_init)
        o = acc_ref[...] / l_ref[...]
        for h in range(DIFF_HEADS):
            o_ref[:, h * DIFF_VDIM:(h + 1) * DIFF_VDIM] = _diff_finish(
                o[h * rows_h:h * rows_h + t], o[h * rows_h + t:(h + 1) * rows_h], lam, g_ref[...], lam_init)


def _sample_attn(page_table, proj, cache_k, cache_v, lam, subln_g, l, lam_init, pp=16):
    n_pages = page_table.shape[1]
    row0 = ROW_SAMPLE0 // DEC_SEQ
    nq = NSUB * DEC_SEQ
    qkv_spec = lambda c: pl.BlockSpec((DEC_SEQ, DIFF_WIDTH), lambda b, g, pt: (row0 + b, c))
    page_spec = lambda i: pl.BlockSpec((None, None, PAGE_ROWS, HEAD_DIM),
                                       lambda b, g, pt: (l, pt[b, g * pp + i], 0, 0))
    grid_spec = pltpu.PrefetchScalarGridSpec(
        num_scalar_prefetch=1,
        grid=(DEC_BATCH, n_pages // pp),
        in_specs=[qkv_spec(0), qkv_spec(1), qkv_spec(2)]
        + [page_spec(i) for i in range(pp)] + [page_spec(i) for i in range(pp)]
        + [pl.BlockSpec((None, 4, HEAD_DIM), lambda b, g, pt: (l, 0, 0)),
           pl.BlockSpec((None, 1, DIFF_VDIM), lambda b, g, pt: (l, 0, 0))],
        out_specs=pl.BlockSpec((DEC_SEQ, DIFF_WIDTH), lambda b, g, pt: (b, 0)),
        scratch_shapes=[
            pltpu.VMEM((DIFF_HEADS, 2 * DEC_SEQ, DIFF_VDIM), BF16),
            pltpu.VMEM((nq, 1), F32),
            pltpu.VMEM((nq, 1), F32),
            pltpu.VMEM((nq, DIFF_VDIM), F32),
        ],
    )
    return pl.pallas_call(
        functools.partial(_sample_attn_kernel, pp=pp, lam_init=lam_init),
        grid_spec=grid_spec,
        out_shape=jax.ShapeDtypeStruct((ROWS_SAMPLE, DIFF_WIDTH), F32),
        compiler_params=_cparams(("parallel", "arbitrary")),
        name="sample_attn",
    )(page_table, proj, proj, proj, *([cache_k] * pp), *([cache_v] * pp), lam, subln_g)


def _page_view(cache):
    d, n = cache.shape[:2]
    c = cache.reshape(d, n, PAGE_SIZE, DIFF_HEADS, 2, HEAD_DIM)
    return jnp.transpose(c, (0, 1, 2, 4, 3, 5)).reshape(d, n, PAGE_ROWS, HEAD_DIM)


def _dn_kernel(xq_ref, xk_ref, xv_ref, z_ref, ba_ref, wq_ref, wk_ref, wv_ref, alog_ref, dt_ref, ng_ref,
               sinit_ref, cinit_ref, *rest, nv, hg, into_mix):
    o_ref, sout_ref, cout_ref, xp_ref, s_ref = rest[1:] if into_mix else rest
    c = pl.program_id(2)
    last = c == pl.num_programs(2) - 1
    width = hg * DN_DK
    x_refs = (xq_ref, xk_ref, xv_ref)
    w_refs = (wq_ref, wk_ref, wv_ref)
    heads = range(hg)

    @pl.when(c == 0)
    def _():
        s_ref[...] = sinit_ref[...]
        for p in range(3):
            xp_ref[p, 0:8, :] = cinit_ref[:, p * width:(p + 1) * width]

    ys = []
    for p in range(3):
        xp_ref[p, 8:8 + nv, :] = x_refs[p][...]
        if nv < CHUNK:
            xp_ref[p, 8 + nv:, :] = jnp.zeros((CHUNK - nv, width), F32)
        xpv = xp_ref[p]
        y = xpv[8:, :] * w_refs[p][CONV_K - 1:CONV_K, :]
        for d in range(1, CONV_K):
            y = y + pltpu.roll(xpv, d, 0)[8:, :] * w_refs[p][CONV_K - 1 - d:CONV_K - d, :]
        ys.append(_silu(y))

    @pl.when(last)
    def _():
        for p in range(3):
            cout_ref[:, p * width:(p + 1) * width] = xp_ref[p, nv:nv + 8, :]

    if nv == CHUNK:
        for p in range(3):
            xp_ref[p, 0:8, :] = xp_ref[p, CHUNK:CHUNK + 8, :]

    ba = ba_ref[...]
    if nv < CHUNK:
        ba = jnp.concatenate([ba, jnp.zeros((CHUNK - nv, LANES), F32)], axis=0)
    gates = ba.T[0:2 * DN_HEADS, :]
    valid = lax.broadcasted_iota(jnp.int32, gates.shape, 1) < nv
    is_beta = lax.broadcasted_iota(jnp.int32, gates.shape, 0) < DN_HEADS
    ab = gates + dt_ref[...]
    softplus = jnp.maximum(ab, 0.0) + jnp.log1p(jnp.exp(-jnp.abs(ab)))
    g_t = jnp.where(jnp.logical_and(valid, jnp.logical_not(is_beta)), -jnp.exp(alog_ref[...]) * softplus, 0.0)
    ri = lax.broadcasted_iota(jnp.int32, (CHUNK, CHUNK), 0)
    ci = lax.broadcasted_iota(jnp.int32, (CHUNK, CHUNK), 1)
    causal = ri >= ci
    eye = (ri == ci).astype(F32)
    cum_t = jnp.dot(g_t, (ri <= ci).astype(F32), preferred_element_type=F32, precision=lax.Precision.HIGHEST)
    slab = jnp.where(is_beta, jnp.where(valid, jax.nn.sigmoid(gates), 0.0), cum_t)
    cols = jnp.concatenate([slab, jnp.zeros((CHUNK - 2 * DN_HEADS, CHUNK), F32)], axis=0).T

    z = z_ref[...]
    if nv < CHUNK:
        z = jnp.concatenate([z, jnp.zeros((CHUNK - nv, width), F32)], axis=0)

    sl = [slice(j * DN_DK, (j + 1) * DN_DK) for j in heads]
    beta = [cols[:, j:j + 1] for j in heads]
    gc = [cols[:, DN_HEADS + j:DN_HEADS + j + 1] for j in heads]
    gr = [cum_t[DN_HEADS + j:DN_HEADS + j + 1, :] for j in heads]
    g_last = [gc[j][CHUNK - 1:CHUNK, :] for j in heads]

    def l2n(x, scale):
        return x * (lax.rsqrt(jnp.sum(x * x, axis=-1, keepdims=True) + EPS) * scale)

    q = [l2n(ys[0][:, sl[j]], DN_DK ** -0.5) for j in heads]
    k = [l2n(ys[1][:, sl[j]], 1.0) for j in heads]
    v = [ys[2][:, sl[j]] for j in heads]
    k_t = [k[j].T for j in heads]
    s_old = [s_ref[j] for j in heads]
    qk16 = [jnp.concatenate([q[j], k[j]], axis=0).astype(BF16) for j in heads]
    a1 = [jnp.dot(qk16[j], k_t[j].astype(BF16), preferred_element_type=F32) for j in heads]
    a2 = [jnp.dot(qk16[j], s_old[j].astype(BF16), preferred_element_type=F32) for j in heads]
    decay = [jnp.exp(jnp.where(causal, gc[j] - gr[j], NEG)) for j in heads]
    e_g = [jnp.exp(gc[j]) for j in heads]
    rhs = [(beta[j] * (v[j] - e_g[j] * a2[j][CHUNK:])).astype(BF16) for j in heads]
    xpow = [a1[j][CHUNK:] * ((decay[j] - eye) * (-beta[j])) for j in heads]
    t_inv = [eye + xpow[j] for j in heads]
    xb = [xpow[j].astype(BF16) for j in heads]
    for _ in range(int(math.log2(CHUNK)) - 1):
        xb = [jnp.dot(xb[j], xb[j], preferred_element_type=F32).astype(BF16) for j in heads]
        t_inv = [t_inv[j] + jnp.dot(t_inv[j].astype(BF16), xb[j], preferred_element_type=F32) for j in heads]
    u = [jnp.dot(t_inv[j].astype(BF16), rhs[j], preferred_element_type=F32).astype(BF16) for j in heads]
    o = [e_g[j] * a2[j][:CHUNK] + jnp.dot((a1[j][:CHUNK] * decay[j]).astype(BF16), u[j],
                                          preferred_element_type=F32) for j in heads]
    for j in heads:
        tail = jnp.exp(g_last[j] - gr[j])
        s_ref[j] = jnp.exp(g_last[j]) * s_old[j] + jnp.dot((k_t[j] * tail).astype(BF16), u[j],
                                                           preferred_element_type=F32)
    for j in heads:
        zj = z[:, sl[j]]
        on = _rms(o[j], ng_ref[...]) * _silu(zj)
        o_ref[:, sl[j]] = on[:nv].astype(o_ref.dtype)

    @pl.when(last)
    def _():
        sout_ref[...] = s_ref[...]


def _dn(proj, tail, conv_w, alog_rows, dt_rows, ng, s_init, c_init, l, *, nv, row0, nseq, nchunk, shared_init,
        out_dtype=None, mix=None):
    hg = DN_HEADS
    width = hg * DN_DK
    q0 = 3 * DIFF_WIDTH // width
    rb0 = row0 // nv
    row_map = lambda s, g, c: rb0 + s * nchunk + c
    init_map = (lambda s: 0) if shared_init else (lambda s: s)
    x_spec = lambda sec: pl.BlockSpec((nv, width), lambda s, g, c: (row_map(s, g, c), q0 + sec))
    w_spec = lambda sec: pl.BlockSpec((None, CONV_K, width), lambda s, g, c: (l, 0, sec))
    in_specs = [
        x_spec(0), x_spec(1), x_spec(2), x_spec(3),
        pl.BlockSpec((nv, LANES), lambda s, g, c: (row_map(s, g, c), 0)),
        w_spec(0), w_spec(1), w_spec(2),
        pl.BlockSpec((None, 2 * DN_HEADS, LANES), lambda s, g, c: (l, 0, 0)),
        pl.BlockSpec((None, 2 * DN_HEADS, LANES), lambda s, g, c: (l, 0, 0)),
        pl.BlockSpec((None, 1, DN_DK), lambda s, g, c: (l, 0, 0)),
        pl.BlockSpec((None, hg, DN_DK, DN_DK), lambda s, g, c: (init_map(s), 0, 0, 0)),
        pl.BlockSpec((None, 8, 3 * DN_WIDTH), lambda s, g, c: (init_map(s), 0, 0)),
    ]
    args = [proj, proj, proj, proj, tail, conv_w, conv_w, conv_w, alog_rows, dt_rows, ng, s_init, c_init]
    if mix is None:
        o_spec = pl.BlockSpec((nv, width), lambda s, g, c: (s * nchunk + c, 0))
        o_shape = jax.ShapeDtypeStruct((nseq * nchunk * nv, DN_WIDTH), out_dtype)
        aliases = {}
    else:
        o_spec = pl.BlockSpec((nv, width), lambda s, g, c: (row_map(s, g, c), DIFF_WIDTH // width))
        o_shape = jax.ShapeDtypeStruct(mix.shape, mix.dtype)
        aliases = {len(args): 0}
        in_specs.append(pl.BlockSpec(memory_space=pl.ANY))
        args.append(mix)
    return pl.pallas_call(
        functools.partial(_dn_kernel, nv=nv, hg=hg, into_mix=mix is not None),
        grid=(nseq, 1, nchunk),
        in_specs=in_specs,
        input_output_aliases=aliases,
        out_specs=[
            o_spec,
            pl.BlockSpec((None, hg, DN_DK, DN_DK), lambda s, g, c: (s, 0, 0, 0)),
            pl.BlockSpec((None, 8, 3 * DN_WIDTH), lambda s, g, c: (s, 0, 0)),
        ],
        out_shape=[
            o_shape,
            jax.ShapeDtypeStruct((nseq, DN_HEADS, DN_DK, DN_DK), F32),
            jax.ShapeDtypeStruct((nseq, 8, 3 * DN_WIDTH), F32),
        ],
        scratch_shapes=[
            pltpu.VMEM((3, CHUNK + 8, width), F32),
            pltpu.VMEM((hg, DN_DK, DN_DK), F32),
        ],
        compiler_params=_cparams(("parallel", "arbitrary", "arbitrary")),
        name=f"deltanet_{nv}",
    )(*args)


def _cast_kernel(w_ref, o_ref):
    o_ref[...] = w_ref[...].astype(BF16)


def _cast_cols(w, cols, tn=1024):
    depth, k = w.shape[:2]
    return pl.pallas_call(
        _cast_kernel,
        grid=(depth, cols // tn),
        in_specs=[pl.BlockSpec((None, k, tn), lambda d, n: (d, 0, n))],
        out_specs=pl.BlockSpec((None, k, tn), lambda d, n: (d, 0, n)),
        out_shape=jax.ShapeDtypeStruct((depth, k, cols), BF16),
        compiler_params=_cparams(("parallel", "parallel")),
        name="cast_cols",
    )(w)


def kernel(x_prompt, x_sample, cache_k, cache_v, state_ssm, state_conv, page_table, meta_tokens, norm1_g, w_in,
           q_norm_g, k_norm_g, diff_lambda, subln_g, conv_w, a_log, dt_bias, dn_norm_g, w_out, norm2_g, w_up,
           w_down):
    cache_k = _page_view(cache_k)
    cache_v = _page_view(cache_v)

    w_main = _cast_cols(w_in, MAIN_COLS)
    w_tail = jnp.pad(w_in[:, :, MAIN_COLS:], ((0, 0), (0, 0), (0, LANES - 2 * DN_HEADS))).astype(BF16)
    w_out_b = w_out.astype(BF16)
    w_up_b = w_up.astype(BF16)
    w_down_b = w_down.astype(BF16)
    vec = lambda a: a.reshape(DEPTH, 1, a.shape[-1])
    gate_rows = lambda a: jnp.broadcast_to(jnp.pad(a, ((0, 0), (DN_HEADS, 0)))[:, :, None],
                                           (DEPTH, 2 * DN_HEADS, LANES))
    alog_rows = gate_rows(a_log)
    dt_rows = gate_rows(dt_bias)
    norm1_g, norm2_g, q_norm_g, k_norm_g = vec(norm1_g), vec(norm2_g), vec(q_norm_g), vec(k_norm_g)
    subln_g, dn_norm_g = vec(subln_g), vec(dn_norm_g)
    conv_state = jnp.pad(state_conv, ((0, 0), (0, 0), (8 - (CONV_K - 1), 0), (0, 0)))

    x = jnp.concatenate([
        x_prompt.reshape(ROWS_PROMPT, D_MODEL),
        x_sample.reshape(ROWS_SAMPLE, D_MODEL),
        meta_tokens,
        jnp.zeros((N_ROWS - ROW_META0 - N_META, D_MODEL), F32),
    ], axis=0)

    zero_state = jnp.zeros((1, DN_HEADS, DN_DK, DN_DK), F32)
    zero_conv = jnp.zeros((1, 8, 3 * DN_WIDTH), F32)
    outs = [[] for _ in range(8)]
    for l in range(DEPTH):
        lam_init = 0.8 - 0.6 * math.exp(-0.3 * l)
        proj, tail, k2, v2 = _in_proj(x, norm1_g, w_main, w_tail, q_norm_g, k_norm_g, l)
        k_all = proj[:, DIFF_WIDTH:2 * DIFF_WIDTH]
        v_all = proj[:, 2 * DIFF_WIDTH:3 * DIFF_WIDTH]
        meta = slice(ROW_META0, ROW_META0 + N_META)
        kmeta = jnp.pad(k_all[meta], ((0, LANES - N_META), (0, 0)))
        vmeta = jnp.pad(v_all[meta], ((0, LANES - N_META), (0, 0)))

        att_p = _prompt_attn(proj, kmeta, vmeta, diff_lambda, subln_g, l, lam_init)
        att_m = _meta_attn(proj[meta, :DIFF_WIDTH], kmeta, vmeta, diff_lambda, subln_g, l, lam_init)
        att_s = _sample_attn(page_table, proj, cache_k, cache_v, diff_lambda, subln_g, l, lam_init)

        dn_args = (proj, tail, conv_w, alog_rows, dt_rows, dn_norm_g)
        dn_m, s_meta, c_meta = _dn(*dn_args, zero_state, zero_conv, l, nv=N_META, row0=ROW_META0, nseq=1, nchunk=1,
                                   shared_init=True, out_dtype=F32)
        dn_s, s_samp, c_samp = _dn(*dn_args, state_ssm[l], conv_state[l], l, nv=DEC_SEQ, row0=ROW_SAMPLE0,
                                   nseq=DEC_BATCH, nchunk=1, shared_init=False, out_dtype=F32)
        mix, s_prom, c_prom = _dn(*dn_args, s_meta, c_meta, l, nv=CHUNK, row0=0, nseq=BATCH, nchunk=SEQ // CHUNK,
                                  shared_init=True, mix=att_p)
        mix_extra = jnp.concatenate([
            jnp.concatenate([att_s, dn_s], axis=1),
            jnp.concatenate([att_m, dn_m], axis=1),
            jnp.zeros((N_ROWS - ROW_META0 - N_META, MIX_WIDTH), F32),
        ], axis=0).astype(BF16)
        mix = lax.dynamic_update_slice(mix, mix_extra, (ROW_SAMPLE0, 0))
        x = _out_proj(mix, w_out_b, x, l)
        x = _ffn(x, norm2_g, w_up_b, w_down_b, l)

        samp = slice(ROW_SAMPLE0, ROW_SAMPLE0 + ROWS_SAMPLE)
        for t2, prompt_rows, sample_rows in ((k2, outs[0], outs[2]), (v2, outs[1], outs[3])):
            for b in range(BATCH):
                prompt_rows.append(t2[meta.start * NSUB:meta.stop * NSUB])
                prompt_rows.append(t2[b * SEQ * NSUB:(b + 1) * SEQ * NSUB])
            sample_rows.append(t2[samp.start * NSUB:samp.stop * NSUB])
        outs[4].append(s_prom)
        outs[5].append(s_samp)
        outs[6].append(c_prom[:, 8 - (CONV_K - 1):])
        outs[7].append(c_samp[:, 8 - (CONV_K - 1):])

    y_prompt = x[:ROWS_PROMPT].reshape(BATCH, SEQ, D_MODEL)
    y_sample = x[ROW_SAMPLE0:ROW_SAMPLE0 + ROWS_SAMPLE].reshape(DEC_BATCH, DEC_SEQ, D_MODEL)

    def unpaged(rows, lead):
        t = jnp.concatenate(rows, axis=0).reshape(*lead, 2, DIFF_HEADS, HEAD_DIM)
        t = jnp.moveaxis(t, -3, -2)
        return t.reshape(*lead, DIFF_HEADS, DIFF_VDIM)

    kv = [unpaged(outs[i], (DEPTH, BATCH, N_META + SEQ)) for i in (0, 1)]
    kv += [unpaged(outs[i], (DEPTH, DEC_BATCH, DEC_SEQ)) for i in (2, 3)]
    return (y_prompt, y_sample) + tuple(kv) + tuple(jnp.stack(o) for o in outs[4:])
```
